```python
import math, functools
import jax, jax.numpy as jnp
from jax import lax
import numpy as np

D_MODEL = 1024
BATCH = 2
SEQ = 8192
DEPTH = 2

GRID_W = 64
CTX_LEN = 256
N_DIR = 2
N_BRANCH = 4
BRANCH_W = D_MODEL // 2
RET_HEADS = 4
RET_DK = BRANCH_W // RET_HEADS
RET_CHUNK = 128
ROPE_BASE = 10000.0
LRU_W = BRANCH_W
LRU_BLOCKS = 4
LRU_BLOCK = LRU_W // LRU_BLOCKS
LRU_CONV = 4
LRU_C = 8.0
GDN_HEADS = 4
GDN_DK = BRANCH_W // GDN_HEADS
GDN_CHUNK = 64
GDN_CONV = 4
RWKV_N = 64
RWKV_HEADS = BRANCH_W // RWKV_N
RWKV_DECAY_RANK = 64
RWKV_A_RANK = 64
RWKV_GATE_RANK = 128
RWKV_IN = 3 * BRANCH_W + RWKV_GATE_RANK + N_DIR * RWKV_DECAY_RANK + N_DIR * RWKV_A_RANK
D_FF = ((8 * D_MODEL + 3 * 256 - 1) // (3 * 256)) * 256
IN_SPLITS = (("ret", 4 * BRANCH_W), ("lru", 2 * LRU_W), ("gdn_qkv", 3 * BRANCH_W),
             ("gdn_z", BRANCH_W), ("gdn_a", N_DIR * GDN_HEADS), ("gdn_b", N_DIR * GDN_HEADS),
             ("rwkv", RWKV_IN), ("gates", N_BRANCH * D_MODEL))
N_IN = sum(w for _, w in IN_SPLITS)
EPS = 1e-6
RWKV_LN_EPS = 64e-5

kernel_name = "hybrid_gated_parallel_diffusion_block"

F32 = jnp.float32


def rms_norm(x, g=None, eps=EPS):
    xf = x.astype(F32)
    y = xf * lax.rsqrt(jnp.mean(xf * xf, axis=-1, keepdims=True) + eps)
    if g is not None:
        y = y * g.astype(F32)
    return y.astype(x.dtype)


def l2_normalize(x, eps=EPS):
    xf = x.astype(F32)
    return (xf * lax.rsqrt(jnp.sum(xf * xf, axis=-1, keepdims=True) + eps)).astype(x.dtype)


def modulate(x, shift, scale):
    return x * (1.0 + scale) + shift


def centred_dwconv(x, w):
    k = w.shape[0]
    return lax.conv_general_dilated(
        x, w[:, None, :].astype(x.dtype), window_strides=(1,),
        padding=[(k // 2, k - 1 - k // 2)],
        dimension_numbers=("NWC", "WIO", "NWC"), feature_group_count=x.shape[-1])


def centred_shift(x):
    xp = jnp.pad(x, ((0, 0), (1, 1), (0, 0)))
    return 0.5 * (xp[:, :-2] + xp[:, 2:])


def axial_rope(rows, dim):
    t = jnp.arange(rows * GRID_W)
    row = (t // GRID_W).astype(F32)
    col = (t % GRID_W).astype(F32)
    quarter = dim // 4
    inv = ROPE_BASE ** (-jnp.arange(quarter, dtype=F32) / quarter)
    ang = jnp.concatenate([row[:, None] * inv, col[:, None] * inv], axis=-1)
    return jnp.cos(ang), jnp.sin(ang)


def apply_rope(x, cos, sin):
    half = x.shape[-1] // 2
    x1, x2 = x[..., :half], x[..., half:]
    c, s = cos[None, :, None, :], sin[None, :, None, :]
    return jnp.concatenate([x1 * c - x2 * s, x1 * s + x2 * c], axis=-1)


def _flip(arrays):
    return tuple(a[:, ::-1] for a in arrays)


def bidirectional(core_f, core_b, ctx_f, ctx_b, lat_f, lat_b, s0):
    yc_f, sc_f = core_f(ctx_f, s0)
    yl_f, _ = core_f(lat_f, sc_f)
    yc_b, sc_b = core_b(_flip(ctx_b), s0)
    yl_b, _ = core_b(_flip(lat_b), sc_b)
    return yc_f + yc_b[:, ::-1], yl_f + yl_b[:, ::-1]


def retention_core(inputs, s0, log_gamma):
    q, k, v = (a.astype(F32) for a in inputs)
    b, t, h, dk = q.shape
    dv = v.shape[-1]
    n = t // RET_CHUNK
    q = q.reshape(b, n, RET_CHUNK, h, dk)
    k = k.reshape(b, n, RET_CHUNK, h, dk)
    v = v.reshape(b, n, RET_CHUNK, h, dv)
    lg = log_gamma.astype(F32)
    pos = jnp.arange(RET_CHUNK, dtype=F32)
    rel = pos[:, None] - pos[None, :]
    lower = rel >= 0
    dmat = jnp.where(lower, jnp.exp(jnp.where(lower, rel, 0.0)[None] * lg[:, None, None]), 0.0)
    scores = jnp.einsum("bnihd,bnjhd->bnhij", q, k) * dmat
    intra = jnp.einsum("bnhij,bnjhe->bnihe", scores, v)
    q_dec = jnp.exp((pos + 1.0)[:, None] * lg)
    k_dec = jnp.exp((RET_CHUNK - 1.0 - pos)[:, None] * lg)
    kv = jnp.einsum("bnjhd,bnjhe->nbhde", k * k_dec[:, :, None], v)
    chunk_decay = jnp.exp(RET_CHUNK * lg)[:, None, None]

    def step(s, kv_n):
        return s * chunk_decay + kv_n, s

    s_final, s_prev = lax.scan(step, s0, kv)
    inter = jnp.einsum("bnihd,nbhde->bnihe", q * q_dec[:, :, None], s_prev)
    return (intra + inter).reshape(b, t, h, dv), s_final


def _lru_combine(left, right):
    a_l, b_l = left
    a_r, b_r = right
    return a_l * a_r, a_r * b_l + b_r


def rglru_core(inputs, h0):
    a, bx = (x.astype(F32) for x in inputs)
    a_cum, h = lax.associative_scan(_lru_combine, (a, bx), axis=1)
    h = h + a_cum * h0[:, None, :]
    return h, h[:, -1]


def gated_delta_core(inputs, s0):
    q, k, v, g, beta = (a.astype(F32) for a in inputs)
    b, t, h, dk = q.shape
    dv = v.shape[-1]
    c = GDN_CHUNK
    n = t // c

    def blocks(a):
        return jnp.moveaxis(a.reshape((b, n, c) + a.shape[2:]), 3, 2)

    q, k, v, g, beta = (blocks(a) for a in (q, k, v, g, beta))
    gc = jnp.cumsum(g, axis=-1)
    pos = jnp.arange(c)
    incl = pos[:, None] >= pos[None, :]
    strict = pos[:, None] > pos[None, :]
    diff = gc[..., :, None] - gc[..., None, :]
    decay = jnp.where(incl, jnp.exp(jnp.where(incl, diff, 0.0)), 0.0)
    kbeta = k * beta[..., None]
    a_low = jnp.where(strict, jnp.einsum("bnhid,bnhjd->bnhij", kbeta, k) * decay, 0.0)
    eye = jnp.eye(c, dtype=F32)
    t_inv = lax.linalg.triangular_solve(a_low + eye, jnp.broadcast_to(eye, a_low.shape),
                                        left_side=True, lower=True, unit_diagonal=True)
    u = t_inv @ (v * beta[..., None])
    w = t_inv @ (kbeta * jnp.exp(gc)[..., None])
    qk = jnp.einsum("bnhid,bnhjd->bnhij", q, k) * decay
    q_dec = q * jnp.exp(gc)[..., None]
    g_last = gc[..., -1]
    k_dec = k * jnp.exp(g_last[..., None] - gc)[..., None]
    xs = tuple(jnp.moveaxis(a, 1, 0) for a in (u, w, qk, q_dec, k_dec, g_last))

    def step(s, x):
        u_n, w_n, qk_n, qd_n, kd_n, gl_n = x
        v_new = u_n - jnp.einsum("bhcd,bhde->bhce", w_n, s)
        o = jnp.einsum("bhcd,bhde->bhce", qd_n, s) + jnp.einsum("bhij,bhje->bhie", qk_n, v_new)
        s = s * jnp.exp(gl_n)[..., None, None] + jnp.einsum("bhcd,bhce->bhde", kd_n, v_new)
        return s, o

    s_final, o = lax.scan(step, s0, xs)
    o = jnp.transpose(o, (1, 0, 3, 2, 4)).reshape(b, t, h, dv)
    return o, s_final


def rwkv7_core(inputs, s0):
    r, logw, k, v, kk, a = (jnp.moveaxis(x.astype(F32), 1, 0) for x in inputs)

    def step(s, x):
        r_t, lw_t, k_t, v_t, kk_t, a_t = x
        sa = jnp.einsum("bhvk,bhk->bhv", s, -kk_t)
        s = (s * jnp.exp(lw_t)[:, :, None, :] + sa[..., None] * (kk_t * a_t)[:, :, None, :]
             + v_t[..., None] * k_t[:, :, None, :])
        return s, jnp.einsum("bhvk,bhk->bhv", s, r_t)

    s_final, y = lax.scan(step, s0, (r, logw, k, v, kk, a))
    return jnp.moveaxis(y, 0, 1), s_final


def retention_branch(p_ctx, p_lat, decay_exp, cos, sin):
    def heads(p):
        q, k, v, g = jnp.split(p, 4, axis=-1)
        shp = p.shape[:2] + (RET_HEADS, RET_DK)
        return q.reshape(shp), k.reshape(shp) * RET_DK ** -0.5, v.reshape(shp), g

    qc, kc, vc, gc = heads(p_ctx)
    ql, kl, vl, gl = heads(p_lat)
    ql, kl = apply_rope(ql, cos, sin), apply_rope(kl, cos, sin)
    log_gamma = jnp.log1p(-jnp.exp2(-decay_exp.astype(F32)))
    core_f = functools.partial(retention_core, log_gamma=log_gamma[0])
    core_b = functools.partial(retention_core, log_gamma=log_gamma[1])
    s0 = jnp.zeros((p_ctx.shape[0], RET_HEADS, RET_DK, RET_DK), F32)
    yc, yl = bidirectional(core_f, core_b, (qc, kc, vc), (qc, kc, vc), (ql, kl, vl), (ql, kl, vl), s0)

    def out(y, g):
        return (rms_norm(y).reshape(g.shape) * jax.nn.silu(g.astype(F32))).astype(g.dtype)

    return out(yc, gc), out(yl, gl)


def rglru_branch(p_ctx, p_lat, conv_w, conv_b, gate_w, gate_b, lam):
    log_sig_lam = jax.nn.log_sigmoid(lam.astype(F32))

    def prep(p):
        xb, yb = jnp.split(p, 2, axis=-1)
        xc = centred_dwconv(xb, conv_w) + conv_b
        bsz, t = xc.shape[:2]
        blk = xc.reshape(bsz, t, LRU_BLOCKS, LRU_BLOCK)
        gates = jnp.einsum("btnc,dgncz->btdgnz", blk, gate_w).reshape(bsz, t, N_DIR, 2, LRU_W) + gate_b
        gates = jax.nn.sigmoid(gates.astype(F32))
        log_a = LRU_C * gates[:, :, :, 0] * log_sig_lam
        a = jnp.exp(log_a)
        bx = jnp.sqrt(-jnp.expm1(2.0 * log_a)) * gates[:, :, :, 1] * xc[:, :, None].astype(F32)
        return (a[:, :, 0], bx[:, :, 0]), (a[:, :, 1], bx[:, :, 1]), yb

    cf, cb, yc = prep(p_ctx)
    lf, lb, yl = prep(p_lat)
    h0 = jnp.zeros((p_ctx.shape[0], LRU_W), F32)
    hc, hl = bidirectional(rglru_core, rglru_core, cf, cb, lf, lb, h0)
    return ((hc * jax.nn.gelu(yc.astype(F32))).astype(p_ctx.dtype),
            (hl * jax.nn.gelu(yl.astype(F32))).astype(p_lat.dtype))


def gated_deltanet_branch(qkv_c, z_c, a_c, b_c, qkv_l, z_l, a_l, b_l, conv_w, a_log, dt_bias, norm_g):
    def prep(qkv, a, b):
        qkv = jax.nn.silu(centred_dwconv(qkv, conv_w))
        q, k, v = jnp.split(qkv, 3, axis=-1)
        shp = qkv.shape[:2] + (GDN_HEADS, GDN_DK)
        q = l2_normalize(q.reshape(shp)) * GDN_DK ** -0.5
        k = l2_normalize(k.reshape(shp))
        v = v.reshape(shp)
        dshp = qkv.shape[:2] + (N_DIR, GDN_HEADS)
        g = -jnp.exp(a_log.astype(F32)) * jax.nn.softplus(a.reshape(dshp).astype(F32) + dt_bias)
        beta = jax.nn.sigmoid(b.reshape(dshp).astype(F32))
        return (q, k, v, g[:, :, 0], beta[:, :, 0]), (q, k, v, g[:, :, 1], beta[:, :, 1])

    cf, cb = prep(qkv_c, a_c, b_c)
    lf, lb = prep(qkv_l, a_l, b_l)
    s0 = jnp.zeros((qkv_c.shape[0], GDN_HEADS, GDN_DK, GDN_DK), F32)
    yc, yl = bidirectional(gated_delta_core, gated_delta_core, cf, cb, lf, lb, s0)

    def out(y, z):
        zh = z.reshape(y.shape).astype(F32)
        return (rms_norm(y, norm_g) * jax.nn.silu(zh)).reshape(z.shape).astype(z.dtype)

    return out(yc, z_c), out(yl, z_l)


def head_layer_norm(y, g, b, eps):
    mu = jnp.mean(y, axis=-1, keepdims=True)
    var = jnp.mean(jnp.square(y - mu), axis=-1, keepdims=True)
    yn = (y - mu) * lax.rsqrt(var + eps)
    return yn.reshape(y.shape[:2] + (-1,)) * g.astype(F32) + b.astype(F32)


def rwkv7_branch(p_ctx, p_lat, mu, w0, w2, a0, a2, g2, k_k, k_a, r_k, ln_g, ln_b):
    bw = BRANCH_W
    cuts = [bw, 2 * bw, 3 * bw, 3 * bw + RWKV_GATE_RANK, 3 * bw + RWKV_GATE_RANK + N_DIR * RWKV_DECAY_RANK]

    def prep(p):
        p = p + (centred_shift(p) - p) * mu
        bsz, t = p.shape[:2]
        r, k, v, gd, wd, ad = jnp.split(p, cuts, axis=-1)
        wd = wd.reshape(bsz, t, N_DIR, RWKV_DECAY_RANK)
        ad = ad.reshape(bsz, t, N_DIR, RWKV_A_RANK)
        w_raw = -jax.nn.softplus(-(w0 + jnp.einsum("btdr,dre->btde", jnp.tanh(wd), w2)).astype(F32)) - 0.5
        logw = -jnp.exp(w_raw)
        a = jax.nn.sigmoid((a0 + jnp.einsum("btdr,dre->btde", ad, a2)).astype(F32))
        g = (jax.nn.sigmoid(gd) @ g2).astype(F32)
        hs = (bsz, t, RWKV_HEADS, RWKV_N)
        kk = l2_normalize((k * k_k).reshape(hs))
        kd = k[:, :, None].astype(F32) * (1.0 + (a - 1.0) * k_a)
        rh, vh = r.reshape(hs), v.reshape(hs)
        fwd = (rh, logw[:, :, 0].reshape(hs), kd[:, :, 0].reshape(hs), vh, kk, a[:, :, 0].reshape(hs))
        bwd = (rh, logw[:, :, 1].reshape(hs), kd[:, :, 1].reshape(hs), vh, kk, a[:, :, 1].reshape(hs))
        ksum = (kd[:, :, 0] + kd[:, :, 1]).reshape(hs)
        bonus = jnp.sum(rh * ksum * r_k, axis=-1, keepdims=True) * vh
        return fwd, bwd, g, bonus.reshape(bsz, t, bw)

    cf, cb, gc, bc = prep(p_ctx)
    lf, lb, gl, bl = prep(p_lat)
    s0 = jnp.zeros((p_ctx.shape[0], RWKV_HEADS, RWKV_N, RWKV_N), F32)
    yc, yl = bidirectional(rwkv7_core, rwkv7_core, cf, cb, lf, lb, s0)

    def out(y, g, bonus, dtype):
        return ((head_layer_norm(y, ln_g, ln_b, RWKV_LN_EPS) + bonus) * g).astype(dtype)

    return out(yc, gc, bc, p_ctx.dtype), out(yl, gl, bl, p_lat.dtype)


def split_projection(p):
    out = {}
    off = 0
    for name, width in IN_SPLITS:
        out[name] = p[..., off:off + width]
        off += width
    return out


def merge_branches(branches, gate_pre, gate_b, branch_w, out_w):
    bsz, t = gate_pre.shape[:2]
    gates = jax.nn.sigmoid((gate_pre + gate_b).astype(F32)).reshape(bsz, t, N_BRANCH, D_MODEL)
    ys = jnp.stack(branches, axis=2)
    proj = jnp.einsum("btkw,kwd->btkd", ys, branch_w)
    merged = jnp.sum(gates.astype(proj.dtype) * proj, axis=2)
    return merged @ out_w


def swiglu(u, w1, w3, w2):
    return (jax.nn.silu(u @ w1) * (u @ w3)) @ w2


def setup_inputs(seed: int = 0) -> dict:
    key = jax.random.key(seed)
    ks = iter(jax.random.split(key, 48))
    L = DEPTH

    def nrm(shape, scale):
        return jax.random.normal(next(ks), shape, F32) * scale

    def unif(shape, lo, hi):
        return jax.random.uniform(next(ks), shape, F32, lo, hi)

    x = nrm((BATCH, SEQ, D_MODEL), 1.0)
    c = nrm((BATCH, D_MODEL), 1.0)
    ctx = nrm((BATCH, CTX_LEN, D_MODEL), 1.0)
    c_ctx = nrm((D_MODEL,), 1.0)
    mod_w = nrm((L, D_MODEL, 6 * D_MODEL), D_MODEL ** -0.5)
    mod_b = nrm((L, 6 * D_MODEL), 0.02)
    norm1_g = 1.0 + nrm((L, D_MODEL), 0.02)
    norm2_g = 1.0 + nrm((L, D_MODEL), 0.02)
    in_w = nrm((L, D_MODEL, N_IN), D_MODEL ** -0.5)
    gate_b = nrm((L, N_BRANCH * D_MODEL), 0.02)
    ret_decay_exp = 5.0 + jnp.arange(RET_HEADS, dtype=F32) + nrm((L, N_DIR, RET_HEADS), 0.1)
    lru_conv_w = nrm((L, LRU_CONV, LRU_W), LRU_CONV ** -0.5)
    lru_conv_b = nrm((L, LRU_W), 0.02)
    lru_gate_w = nrm((L, N_DIR, 2, LRU_BLOCKS, LRU_BLOCK, LRU_BLOCK), LRU_BLOCK ** -0.5)
    lru_gate_b = nrm((L, N_DIR, 2, LRU_W), 0.02)
    s = unif((L, N_DIR, LRU_W), 0.9, 0.999) ** (1.0 / LRU_C)
    lru_lambda = jnp.log(s) - jnp.log1p(-s)
    gdn_conv_w = nrm((L, GDN_CONV, 3 * BRANCH_W), GDN_CONV ** -0.5)
    gdn_a_log = jnp.log(unif((L, N_DIR, GDN_HEADS), 1.0, 16.0))
    dt = jnp.exp(unif((L, N_DIR, GDN_HEADS), math.log(1e-3), math.log(1e-1)))
    gdn_dt_bias = dt + jnp.log(-jnp.expm1(-dt))
    gdn_norm_g = 1.0 + nrm((L, GDN_DK), 0.02)
    rwkv_mu = unif((L, RWKV_IN), 0.0, 1.0)
    ramp = jnp.linspace(0.0, 1.0, BRANCH_W, dtype=F32)
    rwkv_w0 = -6.0 + 5.0 * ramp ** 0.9 + nrm((L, N_DIR, BRANCH_W), 0.1)
    rwkv_w2 = nrm((L, N_DIR, RWKV_DECAY_RANK, BRANCH_W), 0.1 * RWKV_DECAY_RANK ** -0.5)
    rwkv_a0 = nrm((L, N_DIR, BRANCH_W), 0.1)
    rwkv_a2 = nrm((L, N_DIR, RWKV_A_RANK, BRANCH_W), 0.1 * RWKV_A_RANK ** -0.5)
    rwkv_g2 = nrm((L, RWKV_GATE_RANK, BRANCH_W), RWKV_GATE_RANK ** -0.5)
    rwkv_k_k = 0.85 + nrm((L, BRANCH_W), 0.02)
    rwkv_k_a = 1.0 + nrm((L, BRANCH_W), 0.02)
    rwkv_r_k = nrm((L, RWKV_HEADS, RWKV_N), 0.1)
    rwkv_ln_g = 1.0 + nrm((L, BRANCH_W), 0.02)
    rwkv_ln_b = nrm((L, BRANCH_W), 0.02)
    branch_w = nrm((L, N_BRANCH, BRANCH_W, D_MODEL), BRANCH_W ** -0.5)
    out_w = nrm((L, D_MODEL, D_MODEL), D_MODEL ** -0.5)
    ffn_w1 = nrm((L, D_MODEL, D_FF), D_MODEL ** -0.5)
    ffn_w3 = nrm((L, D_MODEL, D_FF), D_MODEL ** -0.5)
    ffn_w2 = nrm((L, D_FF, D_MODEL), D_FF ** -0.5)
    final_norm_g = 1.0 + nrm((D_MODEL,), 0.02)
    return {"x": x, "c": c, "ctx": ctx, "c_ctx": c_ctx, "mod_w": mod_w, "mod_b": mod_b,
            "norm1_g": norm1_g, "norm2_g": norm2_g, "in_w": in_w, "gate_b": gate_b,
            "ret_decay_exp": ret_decay_exp, "lru_conv_w": lru_conv_w, "lru_conv_b": lru_conv_b,
            "lru_gate_w": lru_gate_w, "lru_gate_b": lru_gate_b, "lru_lambda": lru_lambda,
            "gdn_conv_w": gdn_conv_w, "gdn_a_log": gdn_a_log, "gdn_dt_bias": gdn_dt_bias,
            "gdn_norm_g": gdn_norm_g, "rwkv_mu": rwkv_mu, "rwkv_w0": rwkv_w0, "rwkv_w2": rwkv_w2,
            "rwkv_a0": rwkv_a0, "rwkv_a2": rwkv_a2, "rwkv_g2": rwkv_g2, "rwkv_k_k": rwkv_k_k,
            "rwkv_k_a": rwkv_k_a, "rwkv_r_k": rwkv_r_k, "rwkv_ln_g": rwkv_ln_g, "rwkv_ln_b": rwkv_ln_b,
            "branch_w": branch_w, "out_w": out_w, "ffn_w1": ffn_w1, "ffn_w3": ffn_w3,
            "ffn_w2": ffn_w2, "final_norm_g": final_norm_g}


def reference(x, c, ctx, c_ctx, mod_w, mod_b, norm1_g, norm2_g, in_w, gate_b, ret_decay_exp,
              lru_conv_w, lru_conv_b, lru_gate_w, lru_gate_b, lru_lambda, gdn_conv_w, gdn_a_log,
              gdn_dt_bias, gdn_norm_g, rwkv_mu, rwkv_w0, rwkv_w2, rwkv_a0, rwkv_a2, rwkv_g2,
              rwkv_k_k, rwkv_k_a, rwkv_r_k, rwkv_ln_g, rwkv_ln_b, branch_w, out_w, ffn_w1, ffn_w3,
              ffn_w2, final_norm_g):
    rows = x.shape[1] // GRID_W
    cos, sin = axial_rope(rows, RET_DK)
    cond_lat = jax.nn.silu(c)[:, None, :]
    cond_ctx = jax.nn.silu(c_ctx)[None, None, :]
    h_lat, h_ctx = x, ctx
    for l in range(DEPTH):
        last = l == DEPTH - 1
        ml = jnp.split(cond_lat @ mod_w[l] + mod_b[l], 6, axis=-1)
        mc = jnp.split(cond_ctx @ mod_w[l] + mod_b[l], 6, axis=-1)
        pl = split_projection(modulate(rms_norm(h_lat, norm1_g[l]), ml[0], ml[1]) @ in_w[l])
        pc = split_projection(modulate(rms_norm(h_ctx, norm1_g[l]), mc[0], mc[1]) @ in_w[l])

        ret_c, ret_l = retention_branch(pc["ret"], pl["ret"], ret_decay_exp[l], cos, sin)
        lru_c, lru_l = rglru_branch(pc["lru"], pl["lru"], lru_conv_w[l], lru_conv_b[l],
                                    lru_gate_w[l], lru_gate_b[l], lru_lambda[l])
        gdn_c, gdn_l = gated_deltanet_branch(pc["gdn_qkv"], pc["gdn_z"], pc["gdn_a"], pc["gdn_b"],
                                             pl["gdn_qkv"], pl["gdn_z"], pl["gdn_a"], pl["gdn_b"],
                                             gdn_conv_w[l], gdn_a_log[l], gdn_dt_bias[l], gdn_norm_g[l])
        rwkv_c, rwkv_l = rwkv7_branch(pc["rwkv"], pl["rwkv"], rwkv_mu[l], rwkv_w0[l], rwkv_w2[l],
                                      rwkv_a0[l], rwkv_a2[l], rwkv_g2[l], rwkv_k_k[l], rwkv_k_a[l],
                                      rwkv_r_k[l], rwkv_ln_g[l], rwkv_ln_b[l])

        y_lat = merge_branches([ret_l, lru_l, gdn_l, rwkv_l], pl["gates"], gate_b[l], branch_w[l], out_w[l])
        h_lat = h_lat + ml[2] * y_lat
        h_lat = h_lat + ml[5] * swiglu(modulate(rms_norm(h_lat, norm2_g[l]), ml[3], ml[4]),
                                       ffn_w1[l], ffn_w3[l], ffn_w2[l])
        if not last:
            y_ctx = merge_branches([ret_c, lru_c, gdn_c, rwkv_c], pc["gates"], gate_b[l], branch_w[l], out_w[l])
            h_ctx = h_ctx + mc[2] * y_ctx
            h_ctx = h_ctx + mc[5] * swiglu(modulate(rms_norm(h_ctx, norm2_g[l]), mc[3], mc[4]),
                                           ffn_w1[l], ffn_w3[l], ffn_w2[l])
    return rms_norm(h_lat, final_norm_g)
```

```python
import functools

import jax
import jax.numpy as jnp
from jax import lax
from jax.experimental import pallas as pl
from jax.experimental.pallas import tpu as pltpu

F32 = jnp.float32
BF16 = jnp.bfloat16

GRID_W = 64
ROPE_BASE = 10000.0
EPS = 1e-6
RWKV_LN_EPS = 64e-5
LRU_C = 8.0
N_DIR = 2
N_BRANCH = 4
HEAD_DK = 128
N_HEADS = 4
RWKV_N = 64
RWKV_RANK = 64
RWKV_GATE_RANK = 128
CONV_K = 4
LANES = 128
HALO = 8
VMEM_LIMIT_BYTES = 56 * 1024 * 1024

TOKEN_TILE = 256
RET_CHUNK = 128
LRU_BLOCK_ROWS = 256
GDN_CHUNK = 128
RWKV_CHUNK = 64


def _mm(a, b):
    return jnp.dot(a.astype(BF16), b.astype(BF16), preferred_element_type=F32)


def _mm_nt(a, b):
    return lax.dot_general(a.astype(BF16), b.astype(BF16), (((1,), (1,)), ((), ())),
                           preferred_element_type=F32)


def _mm_tn(a, b):
    return lax.dot_general(a.astype(BF16), b.astype(BF16), (((0,), (0,)), ((), ())),
                           preferred_element_type=F32)


def _split3(x):
    x1 = x.astype(BF16)
    r1 = x - x1.astype(F32)
    x2 = r1.astype(BF16)
    x3 = (r1 - x2.astype(F32)).astype(BF16)
    return x1, x2, x3


def _tri_mm(tri, x):
    x1, x2, x3 = _split3(x)
    t = tri.astype(BF16)
    return (jnp.dot(t, x1, preferred_element_type=F32) + jnp.dot(t, x2, preferred_element_type=F32)
            + jnp.dot(t, x3, preferred_element_type=F32))


def _mm_tri(x, tri):
    x1, x2, x3 = _split3(x)
    t = tri.astype(BF16)
    return (jnp.dot(x1, t, preferred_element_type=F32) + jnp.dot(x2, t, preferred_element_type=F32)
            + jnp.dot(x3, t, preferred_element_type=F32))


def _sigmoid(x):
    return 1.0 / (1.0 + jnp.exp(-x))


def _silu(x):
    return x * _sigmoid(x)


def _softplus(x):
    return jnp.maximum(x, 0.0) + jnp.log1p(jnp.exp(-jnp.abs(x)))


def _gelu_tanh(x):
    return 0.5 * x * (1.0 + jnp.tanh(0.7978845608028654 * (x + 0.044715 * x * x * x)))


def _iota2(shape, axis):
    return lax.broadcasted_iota(jnp.int32, shape, axis)


def _scan_block(s, n_ctx, n_all, reverse):
    if not reverse:
        return s
    return jnp.where(s < n_ctx, n_ctx - 1 - s, n_all - 1 - (s - n_ctx))


def _tri_inverse(a, n_max):
    n = a.shape[0]
    ii = _iota2((n, n), 0)
    jj = _iota2((n, n), 1)
    t = jnp.where(ii == jj, 1.0, 0.0) - jnp.where((ii >> 1) == (jj >> 1), a, 0.0)
    b = 2
    while b < n_max:
        sh = b.bit_length() - 1
        off = jnp.where(((ii >> (sh + 1)) == (jj >> (sh + 1))) & ((ii >> sh) != (jj >> sh)), a, 0.0)
        t = t - _mm(_mm(t, off), t)
        b *= 2
    return t


def _fill_halo(buf_ref, x, prev, nxt, row0, rows, ctx_len, total):
    prev_ok = jnp.logical_and(row0 != 0, row0 != ctx_len)
    next_ok = jnp.logical_and(row0 + rows != ctx_len, row0 + rows != total)
    buf_ref[0:HALO, :] = jnp.where(prev_ok, prev, 0.0)
    buf_ref[HALO:HALO + rows, :] = x
    buf_ref[HALO + rows:HALO + rows + HALO, :] = jnp.where(next_ok, nxt, 0.0)


def _shifted(buf_ref, off, rows):
    return buf_ref[HALO + off:HALO + off + rows, :]


def _norm_modulate(x, g, mod_ref, tile, rows, ctx_len, shift_idx, scale_idx):
    xn = x * lax.rsqrt(jnp.mean(x * x, axis=-1, keepdims=True) + EPS) * g
    is_ctx = (tile * rows + _iota2((rows, 1), 0)) < ctx_len
    shift = jnp.where(is_ctx, mod_ref[0, 0, shift_idx:shift_idx + 1, :], mod_ref[0, 1, shift_idx:shift_idx + 1, :])
    scale = jnp.where(is_ctx, mod_ref[0, 0, scale_idx:scale_idx + 1, :], mod_ref[0, 1, scale_idx:scale_idx + 1, :])
    return xn * (1.0 + scale) + shift


def _mod_row(mod_ref, tile, rows, ctx_len, idx):
    is_ctx = (tile * rows + _iota2((rows, 1), 0)) < ctx_len
    return jnp.where(is_ctx, mod_ref[0, 0, idx:idx + 1, :], mod_ref[0, 1, idx:idx + 1, :])


def _params(n_axes=2):
    return pltpu.CompilerParams(dimension_semantics=("arbitrary",) * n_axes,
                                vmem_limit_bytes=VMEM_LIMIT_BYTES)


def _const_spec(shape):
    nd = len(shape)
    return pl.BlockSpec(shape, lambda *_: (0,) * nd)


def _mod_kernel(c_ref, w_ref, b_ref, o_ref):
    o_ref[0] = _mm(_silu(c_ref[...]), w_ref[0]) + b_ref[0]


def _modulation(cond, mod_w, mod_b):
    depth, d, n = mod_w.shape
    tn = n // 4
    return pl.pallas_call(
        _mod_kernel,
        grid=(depth, n // tn),
        in_specs=[pl.BlockSpec(cond.shape, lambda l, j: (0, 0)),
                  pl.BlockSpec((1, d, tn), lambda l, j: (l, 0, j)),
                  pl.BlockSpec((1, 1, tn), lambda l, j: (l, 0, j))],
        out_specs=pl.BlockSpec((1, cond.shape[0], tn), lambda l, j: (l, 0, j)),
        out_shape=jax.ShapeDtypeStruct((depth, cond.shape[0], n), F32),
        compiler_params=_params(),
        name="modulation",
    )(cond, mod_w, mod_b.reshape(depth, 1, n))


def _inproj_kernel(h_ref, g_ref, mod_ref, w_ref, *rest, rows, ctx_len, with_t):
    if with_t:
        wt_ref, o_ref, ot_ref = rest
    else:
        (o_ref,) = rest
    xm = _norm_modulate(h_ref[0], g_ref[...], mod_ref, pl.program_id(1), rows, ctx_len, 0, 1).astype(BF16)
    o_ref[0] = jnp.dot(xm, w_ref[...], preferred_element_type=F32)
    if with_t:
        ot_ref[0] = lax.dot_general(wt_ref[...], xm, (((1,), (1,)), ((), ())), preferred_element_type=F32)


def _in_projection(h, g, modtab, w, ctx_len, w_t=None):
    bsz, total, d = h.shape
    n = w.shape[1]
    rows = TOKEN_TILE
    in_specs = [pl.BlockSpec((1, rows, d), lambda b, i: (b, i, 0)),
                _const_spec((1, d)),
                pl.BlockSpec((1, 2, 6, d), lambda b, i: (b, 0, 0, 0)),
                _const_spec((d, n))]
    out_specs = [pl.BlockSpec((1, rows, n), lambda b, i: (b, i, 0))]
    out_shape = [jax.ShapeDtypeStruct((bsz, total, n), F32)]
    args = [h, g.reshape(1, d), modtab, w]
    if w_t is not None:
        nt = w_t.shape[0]
        in_specs.append(_const_spec((nt, d)))
        out_specs.append(pl.BlockSpec((1, nt, rows), lambda b, i: (b, 0, i)))
        out_shape.append(jax.ShapeDtypeStruct((bsz, nt, total), F32))
        args.append(w_t)
    out = pl.pallas_call(
        functools.partial(_inproj_kernel, rows=rows, ctx_len=ctx_len, with_t=w_t is not None),
        grid=(bsz, total // rows),
        in_specs=in_specs, out_specs=out_specs, out_shape=out_shape,
        compiler_params=_params(),
        name="in_projection",
    )(*args)
    return out if w_t is not None else out[0]


def _retention_kernel(p_ref, cos_ref, sin_ref, lg_ref, *rest, reverse):
    if reverse:
        yf_ref, o_ref, s_ref = rest
    else:
        o_ref, s_ref = rest
    c = RET_CHUNK

    @pl.when(pl.program_id(1) == 0)
    def _():
        s_ref[...] = jnp.zeros_like(s_ref)

    lg = lg_ref[...]
    pos = _iota2((c, 1), 0).astype(F32)
    q_exp = (c - pos) if reverse else (pos + 1.0)
    k_exp = pos if reverse else (c - 1.0 - pos)
    ii = _iota2((c, c), 0)
    jj = _iota2((c, c), 1)
    rel = (jj - ii) if reverse else (ii - jj)
    keep = rel >= 0
    relf = jnp.where(keep, rel, 0).astype(F32)
    cos = cos_ref[...]
    sin = sin_ref[...]
    w = N_HEADS * HEAD_DK
    for h in range(N_HEADS):
        sl = slice(h * HEAD_DK, (h + 1) * HEAD_DK)
        lgh = lg[:, sl]
        q = p_ref[0, :, sl]
        k = p_ref[0, :, w + h * HEAD_DK:w + (h + 1) * HEAD_DK] * (HEAD_DK ** -0.5)
        v = p_ref[0, :, 2 * w + h * HEAD_DK:2 * w + (h + 1) * HEAD_DK]
        q = q * cos + pltpu.roll(q, HEAD_DK // 2, 1) * sin
        k = k * cos + pltpu.roll(k, HEAD_DK // 2, 1) * sin
        dmat = jnp.where(keep, jnp.exp(relf * lgh), 0.0)
        scores = _mm_nt(q, k) * dmat
        state = s_ref[h]
        y = _mm(scores, v) + _mm(q * jnp.exp(q_exp * lgh), state)
        s_ref[h] = state * jnp.exp(c * lgh) + _mm_tn(k * jnp.exp(k_exp * lgh), v)
        if reverse:
            y = y + yf_ref[0, :, sl]
            g = p_ref[0, :, 3 * w + h * HEAD_DK:3 * w + (h + 1) * HEAD_DK]
            y = y * lax.rsqrt(jnp.mean(y * y, axis=-1, keepdims=True) + EPS) * _silu(g)
        o_ref[0, :, sl] = y


def _retention(p, cos, sin, log_gamma_rows, ctx_len):
    bsz, total, _ = p.shape
    w = N_HEADS * HEAD_DK
    c = RET_CHUNK
    n_all, n_ctx = total // c, ctx_len // c
    y = None
    for reverse in (False, True):
        def blk(b, s, reverse=reverse):
            return (b, _scan_block(s, n_ctx, n_all, reverse), 0)

        def tab(b, s, reverse=reverse):
            return (_scan_block(s, n_ctx, n_all, reverse), 0)

        in_specs = [pl.BlockSpec((1, c, 4 * w), blk),
                    pl.BlockSpec((c, HEAD_DK), tab),
                    pl.BlockSpec((c, HEAD_DK), tab),
                    _const_spec((1, w))]
        args = [p, cos, sin, log_gamma_rows[int(reverse)]]
        if reverse:
            in_specs.append(pl.BlockSpec((1, c, w), blk))
            args.append(y)
        y = pl.pallas_call(
            functools.partial(_retention_kernel, reverse=reverse),
            grid=(bsz, n_all),
            in_specs=in_specs,
            out_specs=pl.BlockSpec((1, c, w), blk),
            out_shape=jax.ShapeDtypeStruct((bsz, total, w), F32),
            scratch_shapes=[pltpu.VMEM((N_HEADS, HEAD_DK, HEAD_DK), F32)],
            compiler_params=_params(),
            name="retention_bwd" if reverse else "retention_fwd",
        )(*args)
    return y


def _lru_kernel(x_ref, xp_ref, xn_ref, cw_ref, cb_ref, gw_ref, gb_ref, lam_ref, *rest,
                reverse, ctx_len, total, n_ctx, n_all):
    if reverse:
        hf_ref, o_ref, buf_ref, carry_ref = rest
    else:
        o_ref, buf_ref, carry_ref = rest
    rows = LRU_BLOCK_ROWS
    s = pl.program_id(1)
    blk = _scan_block(s, n_ctx, n_all, reverse)

    @pl.when(s == 0)
    def _():
        carry_ref[...] = jnp.zeros_like(carry_ref)

    w = x_ref.shape[2] // 2
    _fill_halo(buf_ref, x_ref[0, :, 0:w], xp_ref[0], xn_ref[0], blk * rows, rows, ctx_len, total)
    xc = cb_ref[...] + sum(cw_ref[j:j + 1, :] * _shifted(buf_ref, j - CONV_K // 2, rows) for j in range(CONV_K))
    n_blocks = gw_ref.shape[0]
    bw = w // n_blocks
    parts = [_mm(xc[:, n * bw:(n + 1) * bw], gw_ref[n]) for n in range(n_blocks)]
    r_gate = _sigmoid(jnp.concatenate([p[:, :bw] for p in parts], axis=1) + gb_ref[0:1, :])
    i_gate = _sigmoid(jnp.concatenate([p[:, bw:] for p in parts], axis=1) + gb_ref[1:2, :])
    log_a = LRU_C * r_gate * (-_softplus(-lam_ref[...]))
    a = jnp.exp(log_a)
    b = jnp.sqrt(1.0 - jnp.exp(2.0 * log_a)) * i_gate * xc
    ridx = _iota2((rows, 1), 0)
    d = 1
    while d < rows:
        if reverse:
            a_s, b_s, ok = pltpu.roll(a, rows - d, 0), pltpu.roll(b, rows - d, 0), ridx < rows - d
        else:
            a_s, b_s, ok = pltpu.roll(a, d, 0), pltpu.roll(b, d, 0), ridx >= d
        b = b + a * jnp.where(ok, b_s, 0.0)
        a = a * jnp.where(ok, a_s, 1.0)
        d *= 2
    h = b + a * carry_ref[...]
    carry_ref[...] = h[0:1, :] if reverse else h[rows - 1:rows, :]
    if reverse:
        h = (h + hf_ref[0]) * _gelu_tanh(x_ref[0, :, w:2 * w])
    o_ref[0] = h


def _rglru(p, conv_w, conv_b, gate_w, gate_b, lam, ctx_len):
    bsz, total, w2 = p.shape
    w = w2 // 2
    rows = LRU_BLOCK_ROWS
    n_all, n_ctx = total // rows, ctx_len // rows
    per = rows // HALO
    n8 = total // HALO
    n_blocks, bw = gate_w.shape[2], gate_w.shape[3]
    gw = jnp.transpose(gate_w, (0, 2, 3, 1, 4)).reshape(N_DIR, n_blocks, bw, 2 * bw).astype(BF16)
    h = None
    for reverse in (False, True):
        def blk(b, s, reverse=reverse):
            return (b, _scan_block(s, n_ctx, n_all, reverse), 0)

        def prev(b, s, reverse=reverse):
            return (b, jnp.maximum(_scan_block(s, n_ctx, n_all, reverse) * per - 1, 0), 0)

        def nxt(b, s, reverse=reverse):
            return (b, jnp.minimum((_scan_block(s, n_ctx, n_all, reverse) + 1) * per, n8 - 1), 0)

        d = int(reverse)
        in_specs = [pl.BlockSpec((1, rows, w2), blk),
                    pl.BlockSpec((1, HALO, w), prev),
                    pl.BlockSpec((1, HALO, w), nxt),
                    _const_spec((CONV_K, w)), _const_spec((1, w)),
                    _const_spec((n_blocks, bw, 2 * bw)), _const_spec((2, w)), _const_spec((1, w))]
        args = [p, p, p, conv_w, conv_b.reshape(1, w), gw[d], gate_b[d], lam[d].reshape(1, w)]
        if reverse:
            in_specs.append(pl.BlockSpec((1, rows, w), blk))
            args.append(h)
        h = pl.pallas_call(
            functools.partial(_lru_kernel, reverse=reverse, ctx_len=ctx_len, total=total,
                              n_ctx=n_ctx, n_all=n_all),
            grid=(bsz, n_all),
            in_specs=in_specs,
            out_specs=pl.BlockSpec((1, rows, w), blk),
            out_shape=jax.ShapeDtypeStruct((bsz, total, w), F32),
            scratch_shapes=[pltpu.VMEM((rows + 2 * HALO, w), F32), pltpu.VMEM((1, w), F32)],
            compiler_params=_params(),
            name="rglru_bwd" if reverse else "rglru_fwd",
        )(*args)
    return h


def _gdn_kernel(x_ref, xp_ref, xn_ref, abt_ref, cw_ref, alog_r_ref, dtb_r_ref, alog_c_ref, dtb_c_ref,
                ng_ref, *rest, reverse, ctx_len, total, n_ctx, n_all):
    if reverse:
        yf_ref, o_ref, buf_ref, s_ref = rest
    else:
        o_ref, buf_ref, s_ref = rest
    c = GDN_CHUNK
    s = pl.program_id(1)
    blk = _scan_block(s, n_ctx, n_all, reverse)

    @pl.when(s == 0)
    def _():
        s_ref[...] = jnp.zeros_like(s_ref)

    w = N_HEADS * HEAD_DK
    _fill_halo(buf_ref, x_ref[0, :, 0:3 * w], xp_ref[0], xn_ref[0], blk * c, c, ctx_len, total)
    qkv = _silu(sum(cw_ref[j:j + 1, :] * _shifted(buf_ref, j - CONV_K // 2, c) for j in range(CONV_K)))

    ab = x_ref[0, :, 4 * w:4 * w + LANES]
    abt = abt_ref[0]
    g_col = -jnp.exp(alog_r_ref[...]) * _softplus(ab + dtb_r_ref[...])
    g_row = -jnp.exp(alog_c_ref[...]) * _softplus(abt + dtb_c_ref[...])
    beta_col = _sigmoid(ab)
    ii = _iota2((c, c), 0)
    jj = _iota2((c, c), 1)
    incl = (ii <= jj) if reverse else (ii >= jj)
    strict = (ii < jj) if reverse else (ii > jj)
    gc_col = _tri_mm(jnp.where(incl, 1.0, 0.0), g_col)
    incl_t = (ii >= jj) if reverse else (ii <= jj)
    gc_row = _mm_tri(g_row, jnp.where(incl_t, 1.0, 0.0))
    last = 0 if reverse else c - 1
    for h in range(N_HEADS):
        col = int(reverse) * N_HEADS + h
        gcc = gc_col[:, col:col + 1]
        gcr = gc_row[col:col + 1, :]
        beta = beta_col[:, 2 * N_HEADS + col:2 * N_HEADS + col + 1]
        decay = jnp.where(incl, jnp.exp(jnp.where(incl, gcc - gcr, 0.0)), 0.0)
        q = qkv[:, h * HEAD_DK:(h + 1) * HEAD_DK]
        k = qkv[:, w + h * HEAD_DK:w + (h + 1) * HEAD_DK]
        v = qkv[:, 2 * w + h * HEAD_DK:2 * w + (h + 1) * HEAD_DK]
        q = q * lax.rsqrt(jnp.sum(q * q, axis=-1, keepdims=True) + EPS) * (HEAD_DK ** -0.5)
        k = k * lax.rsqrt(jnp.sum(k * k, axis=-1, keepdims=True) + EPS)
        kbeta = k * beta
        a_mat = jnp.where(strict, _mm_nt(kbeta, k) * decay, 0.0)
        t_inv = _tri_inverse(a_mat, c)
        e_gc = jnp.exp(gcc)
        u = _mm(t_inv, v * beta)
        wm = _mm(t_inv, kbeta * e_gc)
        qk = _mm_nt(q, k) * decay
        g_last = gcc[last:last + 1, :]
        state = s_ref[h]
        v_new = u - _mm(wm, state)
        y = _mm(q * e_gc, state) + _mm(qk, v_new)
        s_ref[h] = state * jnp.exp(g_last) + _mm_tn(k * jnp.exp(g_last - gcc), v_new)
        if reverse:
            y = y + yf_ref[0, :, h * HEAD_DK:(h + 1) * HEAD_DK]
            z = x_ref[0, :, 3 * w + h * HEAD_DK:3 * w + (h + 1) * HEAD_DK]
            y = y * lax.rsqrt(jnp.mean(y * y, axis=-1, keepdims=True) + EPS) * ng_ref[...] * _silu(z)
        o_ref[0, :, h * HEAD_DK:(h + 1) * HEAD_DK] = y


def _gated_deltanet(p, abt, conv_w, a_log, dt_bias, norm_g, ctx_len):
    bsz, total, pw = p.shape
    w = N_HEADS * HEAD_DK
    c = GDN_CHUNK
    n_all, n_ctx = total // c, ctx_len // c
    per = c // HALO
    n8 = total // HALO
    nt = abt.shape[1]
    flat_alog = a_log.reshape(-1)
    flat_dtb = dt_bias.reshape(-1)
    alog_r = jnp.zeros((1, LANES), F32).at[0, :flat_alog.shape[0]].set(flat_alog)
    dtb_r = jnp.zeros((1, LANES), F32).at[0, :flat_dtb.shape[0]].set(flat_dtb)
    alog_c = jnp.zeros((nt, 1), F32).at[:flat_alog.shape[0], 0].set(flat_alog)
    dtb_c = jnp.zeros((nt, 1), F32).at[:flat_dtb.shape[0], 0].set(flat_dtb)
    y = None
    for reverse in (False, True):
        def blk(b, s, reverse=reverse):
            return (b, _scan_block(s, n_ctx, n_all, reverse), 0)

        def blk_t(b, s, reverse=reverse):
            return (b, 0, _scan_block(s, n_ctx, n_all, reverse))

        def prev(b, s, reverse=reverse):
            return (b, jnp.maximum(_scan_block(s, n_ctx, n_all, reverse) * per - 1, 0), 0)

        def nxt(b, s, reverse=reverse):
            return (b, jnp.minimum((_scan_block(s, n_ctx, n_all, reverse) + 1) * per, n8 - 1), 0)

        in_specs = [pl.BlockSpec((1, c, pw), blk),
                    pl.BlockSpec((1, HALO, 3 * w), prev),
                    pl.BlockSpec((1, HALO, 3 * w), nxt),
                    pl.BlockSpec((1, nt, c), blk_t),
                    _const_spec((CONV_K, 3 * w)),
                    _const_spec((1, LANES)), _const_spec((1, LANES)),
                    _const_spec((nt, 1)), _const_spec((nt, 1)),
                    _const_spec((1, HEAD_DK))]
        args = [p, p, p, abt, conv_w, alog_r, dtb_r, alog_c, dtb_c, norm_g.reshape(1, HEAD_DK)]
        if reverse:
            in_specs.append(pl.BlockSpec((1, c, w), blk))
            args.append(y)
        y = pl.pallas_call(
            functools.partial(_gdn_kernel, reverse=reverse, ctx_len=ctx_len, total=total,
                              n_ctx=n_ctx, n_all=n_all),
            grid=(bsz, n_all),
            in_specs=in_specs,
            out_specs=pl.BlockSpec((1, c, w), blk),
            out_shape=jax.ShapeDtypeStruct((bsz, total, w), F32),
            scratch_shapes=[pltpu.VMEM((c + 2 * HALO, 3 * w), F32),
                            pltpu.VMEM((N_HEADS, HEAD_DK, HEAD_DK), F32)],
            compiler_params=_params(),
            name="gdn_bwd" if reverse else "gdn_fwd",
        )(*args)
    return y


def _rwkv_kernel(x_ref, xp_ref, xn_ref, mu_ref, w0_ref, w2_ref, a0_ref, a2_ref, kk_ref, ka_ref, *rest,
                 reverse, ctx_len, total, n_ctx, n_all):
    if reverse:
        g2_ref, rk_ref, lng_ref, lnb_ref, yf_ref, o_ref, buf_ref, s_ref = rest
    else:
        o_ref, buf_ref, s_ref = rest
    c = RWKV_CHUNK
    step = pl.program_id(1)
    blk = _scan_block(step, n_ctx, n_all, reverse)

    @pl.when(step == 0)
    def _():
        s_ref[...] = jnp.zeros_like(s_ref)

    bw = s_ref.shape[0] * LANES
    x = x_ref[0]
    _fill_halo(buf_ref, x, xp_ref[0], xn_ref[0], blk * c, c, ctx_len, total)
    x = x + (0.5 * (_shifted(buf_ref, -1, c) + _shifted(buf_ref, 1, c)) - x) * mu_ref[...]
    r = x[:, 0:bw]
    k = x[:, bw:2 * bw]
    v = x[:, 2 * bw:3 * bw]
    off = 3 * bw
    gd = x[:, off:off + RWKV_GATE_RANK]
    off += RWKV_GATE_RANK
    wd = [x[:, off + d * RWKV_RANK:off + (d + 1) * RWKV_RANK] for d in range(N_DIR)]
    off += N_DIR * RWKV_RANK
    ad = [x[:, off + d * RWKV_RANK:off + (d + 1) * RWKV_RANK] for d in range(N_DIR)]

    hshift = RWKV_N.bit_length() - 1
    head_sum = jnp.where((_iota2((bw, bw), 0) >> hshift) == (_iota2((bw, bw), 1) >> hshift), 1.0, 0.0)

    def per_head_sum(t):
        t1, t2, t3 = _split3(t)
        hs = head_sum.astype(BF16)
        return (jnp.dot(t1, hs, preferred_element_type=F32) + jnp.dot(t2, hs, preferred_element_type=F32)
                + jnp.dot(t3, hs, preferred_element_type=F32))

    def a_of(d):
        return _sigmoid(a0_ref[d:d + 1, :] + _mm(ad[d], a2_ref[d]))

    d = int(reverse)
    w_raw = -_softplus(-(w0_ref[d:d + 1, :] + _mm(jnp.tanh(wd[d]), w2_ref[d]))) - 0.5
    logw = -jnp.exp(w_raw)
    a_dir = a_of(d)
    kkv = k * kk_ref[...]
    kkv = kkv * lax.rsqrt(per_head_sum(kkv * kkv) + EPS)
    kd = k * (1.0 + (a_dir - 1.0) * ka_ref[...])

    ii = _iota2((c, c), 0)
    jj = _iota2((c, c), 1)
    incl_c = (ii <= jj) if reverse else (ii >= jj)
    cum = _tri_mm(jnp.where(incl_c, 1.0, 0.0), logw)
    last = 0 if reverse else c - 1
    cum_tot = cum[last:last + 1, :]
    w_incl = jnp.exp(cum)
    w_inv = jnp.exp(-cum)
    w_rest = jnp.exp(cum_tot - cum)
    alpha_t = -kkv * jnp.exp(cum - logw)
    beta = kkv * a_dir
    beta_t = beta * w_inv
    k_t = kd * w_inv
    r_t = r * w_incl
    beta_h = beta * w_rest
    k_h = kd * w_rest
    w_tot = jnp.exp(cum_tot)

    n2 = 2 * c
    i2 = _iota2((n2, n2), 0)
    j2 = _iota2((n2, n2), 1)
    incl = (i2 <= j2) if reverse else (i2 >= j2)
    strict = (i2 < j2) if reverse else (i2 > j2)
    lane = _iota2((1, LANES), 1)
    m0 = jnp.where(lane < RWKV_N, 1.0, 0.0)
    m1 = 1.0 - m0

    def stack(t):
        return jnp.concatenate([t * m0, t * m1], axis=0)

    ys = []
    for p in range(bw // LANES):
        sl = slice(p * LANES, (p + 1) * LANES)
        a_s, r_s = stack(alpha_t[:, sl]), stack(r_t[:, sl])
        b_s, k_s, v_s = stack(beta_t[:, sl]), stack(k_t[:, sl]), stack(v[:, sl])
        m_ab = jnp.where(strict, _mm_nt(a_s, b_s), 0.0)
        m_ak = jnp.where(strict, _mm_nt(a_s, k_s), 0.0)
        p_rb = jnp.where(incl, _mm_nt(r_s, b_s), 0.0)
        p_rk = jnp.where(incl, _mm_nt(r_s, k_s), 0.0)
        t_inv = _tri_inverse(-m_ab, c)
        state = s_ref[p]
        u_s = _mm(t_inv, _mm_nt(a_s, state) + _mm(m_ak, v_s))
        y_s = _mm_nt(r_s, state) + _mm(p_rb, u_s) + _mm(p_rk, v_s)
        ys.append(y_s[0:c, :] + y_s[c:n2, :])
        s_ref[p] = (state * w_tot[:, sl] + _mm_tn(u_s, stack(beta_h[:, sl])) + _mm_tn(v_s, stack(k_h[:, sl])))
    y = jnp.concatenate(ys, axis=1)
    if reverse:
        y = y + yf_ref[0]
        mean = per_head_sum(y) * (1.0 / RWKV_N)
        yc = y - mean
        var = per_head_sum(yc * yc) * (1.0 / RWKV_N)
        yn = yc * lax.rsqrt(var + RWKV_LN_EPS) * lng_ref[...] + lnb_ref[...]
        k_sum = kd + k * (1.0 + (a_of(0) - 1.0) * ka_ref[...])
        bonus = per_head_sum(r * k_sum * rk_ref[...]) * v
        y = (yn + bonus) * _mm(_sigmoid(gd), g2_ref[...])
    o_ref[0] = y


def _rwkv7(p, mu, w0, w2, a0, a2, g2, k_k, k_a, r_k, ln_g, ln_b, ctx_len):
    bsz, total, pw = p.shape
    bw = k_k.shape[0]
    c = RWKV_CHUNK
    n_all, n_ctx = total // c, ctx_len // c
    per = c // HALO
    n8 = total // HALO
    y = None
    for reverse in (False, True):
        def blk(b, s, reverse=reverse):
            return (b, _scan_block(s, n_ctx, n_all, reverse), 0)

        def prev(b, s, reverse=reverse):
            return (b, jnp.maximum(_scan_block(s, n_ctx, n_all, reverse) * per - 1, 0), 0)

        def nxt(b, s, reverse=reverse):
            return (b, jnp.minimum((_scan_block(s, n_ctx, n_all, reverse) + 1) * per, n8 - 1), 0)

        in_specs = [pl.BlockSpec((1, c, pw), blk),
                    pl.BlockSpec((1, HALO, pw), prev),
                    pl.BlockSpec((1, HALO, pw), nxt),
                    _const_spec((1, pw)),
                    _const_spec((N_DIR, bw)), _const_spec((N_DIR, RWKV_RANK, bw)),
                    _const_spec((N_DIR, bw)), _const_spec((N_DIR, RWKV_RANK, bw)),
                    _const_spec((1, bw)), _const_spec((1, bw))]
        args = [p, p, p, mu.reshape(1, pw), w0, w2.astype(BF16), a0, a2.astype(BF16),
                k_k.reshape(1, bw), k_a.reshape(1, bw)]
        if reverse:
            in_specs += [_const_spec((RWKV_GATE_RANK, bw)), _const_spec((1, bw)), _const_spec((1, bw)),
                         _const_spec((1, bw)), pl.BlockSpec((1, c, bw), blk)]
            args += [g2.astype(BF16), r_k.reshape(1, bw), ln_g.reshape(1, bw), ln_b.reshape(1, bw), y]
        y = pl.pallas_call(
            functools.partial(_rwkv_kernel, reverse=reverse, ctx_len=ctx_len, total=total,
                              n_ctx=n_ctx, n_all=n_all),
            grid=(bsz, n_all),
            in_specs=in_specs,
            out_specs=pl.BlockSpec((1, c, bw), blk),
            out_shape=jax.ShapeDtypeStruct((bsz, total, bw), F32),
            scratch_shapes=[pltpu.VMEM((c + 2 * HALO, pw), F32),
                            pltpu.VMEM((bw // LANES, LANES, LANES), F32)],
            compiler_params=_params(),
            name="rwkv_bwd" if reverse else "rwkv_fwd",
        )(*args)
    return y


def _merge_kernel(h_ref, g_ref, mod_ref, wg_ref, gb_ref, y0_ref, y1_ref, y2_ref, y3_ref, bw_ref, ow_ref,
                  o_ref, *, rows, ctx_len):
    tile = pl.program_id(1)
    h = h_ref[0]
    d = h.shape[1]
    xm = _norm_modulate(h, g_ref[...], mod_ref, tile, rows, ctx_len, 0, 1).astype(BF16)
    merged = jnp.zeros_like(h)
    for k, y_ref in enumerate((y0_ref, y1_ref, y2_ref, y3_ref)):
        gate = _sigmoid(jnp.dot(xm, wg_ref[:, k * d:(k + 1) * d], preferred_element_type=F32)
                        + gb_ref[:, k * d:(k + 1) * d])
        merged = merged + gate * _mm(y_ref[0], bw_ref[k])
    o_ref[0] = h + _mod_row(mod_ref, tile, rows, ctx_len, 2) * _mm(merged, ow_ref[...])


def _merge(h, g, modtab, w_gates, gate_b, ys, branch_w, out_w, ctx_len):
    bsz, total, d = h.shape
    rows = TOKEN_TILE
    bwid = ys[0].shape[2]
    row_spec = lambda width: pl.BlockSpec((1, rows, width), lambda b, i: (b, i, 0))
    return pl.pallas_call(
        functools.partial(_merge_kernel, rows=rows, ctx_len=ctx_len),
        grid=(bsz, total // rows),
        in_specs=[row_spec(d), _const_spec((1, d)),
                  pl.BlockSpec((1, 2, 6, d), lambda b, i: (b, 0, 0, 0)),
                  _const_spec((d, N_BRANCH * d)), _const_spec((1, N_BRANCH * d)),
                  row_spec(bwid), row_spec(bwid), row_spec(bwid), row_spec(bwid),
                  _const_spec((N_BRANCH, bwid, d)), _const_spec((d, d))],
        out_specs=row_spec(d),
        out_shape=jax.ShapeDtypeStruct((bsz, total, d), F32),
        compiler_params=_params(),
        name="merge",
    )(h, g.reshape(1, d), modtab, w_gates, gate_b.reshape(1, -1), *ys, branch_w, out_w)


def _ffn_kernel(h_ref, g_ref, mod_ref, w1_ref, w3_ref, w2_ref, fg_ref, o_ref, *, rows, ctx_len, final):
    tile = pl.program_id(1)
    h = h_ref[0]
    u = _norm_modulate(h, g_ref[...], mod_ref, tile, rows, ctx_len, 3, 4).astype(BF16)
    t = _silu(jnp.dot(u, w1_ref[...], preferred_element_type=F32)) * jnp.dot(u, w3_ref[...],
                                                                             preferred_element_type=F32)
    h = h + _mod_row(mod_ref, tile, rows, ctx_len, 5) * _mm(t, w2_ref[...])
    if final:
        h = h * lax.rsqrt(jnp.mean(h * h, axis=-1, keepdims=True) + EPS) * fg_ref[...]
    o_ref[0] = h


def _ffn(h, g, modtab, w1, w3, w2, final_g, ctx_len, final):
    bsz, total, d = h.shape
    rows = TOKEN_TILE
    dff = w1.shape[1]
    row_spec = pl.BlockSpec((1, rows, d), lambda b, i: (b, i, 0))
    return pl.pallas_call(
        functools.partial(_ffn_kernel, rows=rows, ctx_len=ctx_len, final=final),
        grid=(bsz, total // rows),
        in_specs=[row_spec, _const_spec((1, d)),
                  pl.BlockSpec((1, 2, 6, d), lambda b, i: (b, 0, 0, 0)),
                  _const_spec((d, dff)), _const_spec((d, dff)), _const_spec((dff, d)), _const_spec((1, d))],
        out_specs=row_spec,
        out_shape=jax.ShapeDtypeStruct((bsz, total, d), F32),
        compiler_params=_params(),
        name="ffn",
    )(h, g.reshape(1, d), modtab, w1, w3, w2, final_g.reshape(1, d))


def _rope_tables(seq, ctx_len):
    t = jnp.arange(seq)
    row = (t // GRID_W).astype(F32)
    col = (t % GRID_W).astype(F32)
    quarter = HEAD_DK // 4
    inv = ROPE_BASE ** (-jnp.arange(quarter, dtype=F32) / quarter)
    ang = jnp.concatenate([row[:, None] * inv, col[:, None] * inv], axis=-1)
    cos, sin = jnp.cos(ang), jnp.sin(ang)
    cos = jnp.concatenate([cos, cos], axis=-1)
    sin = jnp.concatenate([-sin, sin], axis=-1)
    cos = jnp.concatenate([jnp.ones((ctx_len, HEAD_DK), F32), cos], axis=0)
    sin = jnp.concatenate([jnp.zeros((ctx_len, HEAD_DK), F32), sin], axis=0)
    return cos, sin


def kernel(x, c, ctx, c_ctx, mod_w, mod_b, norm1_g, norm2_g, in_w, gate_b, ret_decay_exp, lru_conv_w,
           lru_conv_b, lru_gate_w, lru_gate_b, lru_lambda, gdn_conv_w, gdn_a_log, gdn_dt_bias, gdn_norm_g,
           rwkv_mu, rwkv_w0, rwkv_w2, rwkv_a0, rwkv_a2, rwkv_g2, rwkv_k_k, rwkv_k_a, rwkv_r_k, rwkv_ln_g,
           rwkv_ln_b, branch_w, out_w, ffn_w1, ffn_w3, ffn_w2, final_norm_g):
    bsz, seq, d = x.shape
    ctx_len = ctx.shape[1]
    depth = mod_w.shape[0]
    bw = d // 2
    assert ctx_len % LRU_BLOCK_ROWS == 0 and seq % LRU_BLOCK_ROWS == 0 and bsz + 1 <= HALO

    h = jnp.concatenate([ctx, x], axis=1)
    cond = jnp.zeros((HALO, d), F32).at[:bsz].set(c).at[bsz].set(c_ctx)
    mods = _modulation(cond, mod_w, mod_b).reshape(depth, HALO, 6, d)
    cos, sin = _rope_tables(seq, ctx_len)

    widths = (4 * bw, 2 * bw, 3 * bw, bw, N_DIR * N_HEADS, N_DIR * N_HEADS,
              3 * bw + RWKV_GATE_RANK + 2 * N_DIR * RWKV_RANK, N_BRANCH * d)
    offs = [0]
    for wdt in widths:
        offs.append(offs[-1] + wdt)

    for l in range(depth):
        lat = mods[l, :bsz]
        modtab = jnp.stack([jnp.broadcast_to(mods[l, bsz], lat.shape), lat], axis=1)
        wl = in_w[l]
        w_ret = wl[:, offs[0]:offs[1]].astype(BF16)
        w_lru = wl[:, offs[1]:offs[2]].astype(BF16)
        w_ab = wl[:, offs[4]:offs[6]]
        w_gdn = jnp.concatenate([wl[:, offs[2]:offs[4]], w_ab,
                                 jnp.zeros((d, LANES - w_ab.shape[1]), F32)], axis=1).astype(BF16)
        w_abt = w_ab.T.astype(BF16)
        w_rwkv = wl[:, offs[6]:offs[7]].astype(BF16)
        w_gates = wl[:, offs[7]:offs[8]].astype(BF16)

        p_ret = _in_projection(h, norm1_g[l], modtab, w_ret, ctx_len)
        p_lru = _in_projection(h, norm1_g[l], modtab, w_lru, ctx_len)
        p_gdn, abt = _in_projection(h, norm1_g[l], modtab, w_gdn, ctx_len, w_t=w_abt)
        p_rwkv = _in_projection(h, norm1_g[l], modtab, w_rwkv, ctx_len)

        log_gamma = jnp.log1p(-jnp.exp2(-ret_decay_exp[l].astype(F32)))
        lg_rows = jnp.repeat(log_gamma, HEAD_DK, axis=1).reshape(N_DIR, 1, N_HEADS * HEAD_DK)
        y_ret = _retention(p_ret, cos, sin, lg_rows, ctx_len)
        y_lru = _rglru(p_lru, lru_conv_w[l], lru_conv_b[l], lru_gate_w[l], lru_gate_b[l], lru_lambda[l], ctx_len)
        y_gdn = _gated_deltanet(p_gdn, abt, gdn_conv_w[l], gdn_a_log[l], gdn_dt_bias[l], gdn_norm_g[l], ctx_len)
        y_rwkv = _rwkv7(p_rwkv, rwkv_mu[l], rwkv_w0[l], rwkv_w2[l], rwkv_a0[l], rwkv_a2[l], rwkv_g2[l],
                        rwkv_k_k[l], rwkv_k_a[l], rwkv_r_k[l].reshape(-1), rwkv_ln_g[l], rwkv_ln_b[l], ctx_len)

        h = _merge(h, norm1_g[l], modtab, w_gates, gate_b[l], (y_ret, y_lru, y_gdn, y_rwkv),
                   branch_w[l].astype(BF16), out_w[l].astype(BF16), ctx_len)
        h = _ffn(h, norm2_g[l], modtab, ffn_w1[l].astype(BF16), ffn_w3[l].astype(BF16),
                 ffn_w2[l].astype(BF16), final_norm_g, ctx_len, final=(l == depth - 1))
    return h[:, ctx_len:]
```

```python
import functools

import jax
import jax.numpy as jnp
from jax import lax
from jax.experimental import pallas as pl
from jax.experimental.pallas import tpu as pltpu

F32 = jnp.float32
BF16 = jnp.bfloat16

GRID_W = 64
ROPE_BASE = 10000.0
EPS = 1e-6
RWKV_LN_EPS = 64e-5
LRU_C = 8.0
N_DIR = 2
N_BRANCH = 4
HEAD_DK = 128
N_HEADS = 4
RWKV_N = 64
RWKV_RANK = 64
RWKV_GATE_RANK = 128
CONV_K = 4
LANES = 128
HALO = 8
VMEM_LIMIT_BYTES = 56 * 1024 * 1024

TOKEN_TILE = 256
RET_CHUNK = 128
LRU_BLOCK_ROWS = 256
GDN_CHUNK = 128
RWKV_CHUNK = 64


def _mm(a, b):
    return jnp.dot(a.astype(BF16), b.astype(BF16), preferred_element_type=F32)


def _mm_nt(a, b):
    return lax.dot_general(a.astype(BF16), b.astype(BF16), (((1,), (1,)), ((), ())),
                           preferred_element_type=F32)


def _mm_tn(a, b):
    return lax.dot_general(a.astype(BF16), b.astype(BF16), (((0,), (0,)), ((), ())),
                           preferred_element_type=F32)


def _split3(x):
    x1 = x.astype(BF16)
    r1 = x - x1.astype(F32)
    x2 = r1.astype(BF16)
    x3 = (r1 - x2.astype(F32)).astype(BF16)
    return x1, x2, x3


def _tri_mm(tri, x):
    x1, x2, x3 = _split3(x)
    t = tri.astype(BF16)
    return (jnp.dot(t, x1, preferred_element_type=F32) + jnp.dot(t, x2, preferred_element_type=F32)
            + jnp.dot(t, x3, preferred_element_type=F32))


def _mm_tri(x, tri):
    x1, x2, x3 = _split3(x)
    t = tri.astype(BF16)
    return (jnp.dot(x1, t, preferred_element_type=F32) + jnp.dot(x2, t, preferred_element_type=F32)
            + jnp.dot(x3, t, preferred_element_type=F32))


def _sigmoid(x):
    return 1.0 / (1.0 + jnp.exp(-x))


def _silu(x):
    return x * _sigmoid(x)


def _softplus(x):
    return jnp.maximum(x, 0.0) + jnp.log1p(jnp.exp(-jnp.abs(x)))


def _gelu_tanh(x):
    return 0.5 * x * (1.0 + jnp.tanh(0.7978845608028654 * (x + 0.044715 * x * x * x)))


def _iota2(shape, axis):
    return lax.broadcasted_iota(jnp.int32, shape, axis)


def _scan_block(s, n_ctx, n_all, reverse):
    if not reverse:
        return s
    return jnp.where(s < n_ctx, n_ctx - 1 - s, n_all - 1 - (s - n_ctx))


def _tri_inverse(mats, n_max):
    n = mats[0].shape[0]
    ii = _iota2((n, n), 0)
    jj = _iota2((n, n), 1)
    eye = jnp.where(ii == jj, 1.0, 0.0)
    pair = (ii >> 1) == (jj >> 1)
    ts = [eye - jnp.where(pair, a, 0.0) for a in mats]
    b = 2
    while b < n_max:
        sh = b.bit_length() - 1
        between = ((ii >> (sh + 1)) == (jj >> (sh + 1))) & ((ii >> sh) != (jj >> sh))
        xs = [_mm(t, jnp.where(between, a, 0.0)) for t, a in zip(ts, mats)]
        ts = [t - _mm(x, t) for t, x in zip(ts, xs)]
        b *= 2
    return ts


def _fill_halo(buf_ref, x, prev, nxt, row0, rows, ctx_len, total):
    prev_ok = jnp.logical_and(row0 != 0, row0 != ctx_len)
    next_ok = jnp.logical_and(row0 + rows != ctx_len, row0 + rows != total)
    buf_ref[0:HALO, :] = jnp.where(prev_ok, prev, 0.0)
    buf_ref[HALO:HALO + rows, :] = x
    buf_ref[HALO + rows:HALO + rows + HALO, :] = jnp.where(next_ok, nxt, 0.0)


def _shifted(buf_ref, off, rows):
    return buf_ref[HALO + off:HALO + off + rows, :]


def _norm_modulate(x, g, mod_ref, tile, rows, ctx_len, shift_idx, scale_idx):
    xn = x * lax.rsqrt(jnp.mean(x * x, axis=-1, keepdims=True) + EPS) * g
    is_ctx = (tile * rows + _iota2((rows, 1), 0)) < ctx_len
    shift = jnp.where(is_ctx, mod_ref[0, 0, shift_idx:shift_idx + 1, :], mod_ref[0, 1, shift_idx:shift_idx + 1, :])
    scale = jnp.where(is_ctx, mod_ref[0, 0, scale_idx:scale_idx + 1, :], mod_ref[0, 1, scale_idx:scale_idx + 1, :])
    return xn * (1.0 + scale) + shift


def _mod_row(mod_ref, tile, rows, ctx_len, idx):
    is_ctx = (tile * rows + _iota2((rows, 1), 0)) < ctx_len
    return jnp.where(is_ctx, mod_ref[0, 0, idx:idx + 1, :], mod_ref[0, 1, idx:idx + 1, :])


def _params(n_axes=2):
    return pltpu.CompilerParams(dimension_semantics=("arbitrary",) * n_axes,
                                vmem_limit_bytes=VMEM_LIMIT_BYTES)


def _const_spec(shape):
    nd = len(shape)
    return pl.BlockSpec(shape, lambda *_: (0,) * nd)


def _mod_kernel(c_ref, w_ref, b_ref, o_ref):
    o_ref[0] = _mm(_silu(c_ref[...]), w_ref[0]) + b_ref[0]


def _modulation(cond, mod_w, mod_b):
    depth, d, n = mod_w.shape
    tn = n // 4
    return pl.pallas_call(
        _mod_kernel,
        grid=(depth, n // tn),
        in_specs=[pl.BlockSpec(cond.shape, lambda l, j: (0, 0)),
                  pl.BlockSpec((1, d, tn), lambda l, j: (l, 0, j)),
                  pl.BlockSpec((1, 1, tn), lambda l, j: (l, 0, j))],
        out_specs=pl.BlockSpec((1, cond.shape[0], tn), lambda l, j: (l, 0, j)),
        out_shape=jax.ShapeDtypeStruct((depth, cond.shape[0], n), F32),
        compiler_params=_params(),
        name="modulation",
    )(cond, mod_w, mod_b.reshape(depth, 1, n))


def _inproj_kernel(h_ref, g_ref, mod_ref, w_ref, *rest, rows, ctx_len, with_t):
    if with_t:
        wt_ref, o_ref, ot_ref = rest
    else:
        (o_ref,) = rest
    xm = _norm_modulate(h_ref[0], g_ref[...], mod_ref, pl.program_id(1), rows, ctx_len, 0, 1).astype(BF16)
    o_ref[0] = jnp.dot(xm, w_ref[...], preferred_element_type=F32)
    if with_t:
        ot_ref[0] = lax.dot_general(wt_ref[...], xm, (((1,), (1,)), ((), ())), preferred_element_type=F32)


def _in_projection(h, g, modtab, w, ctx_len, w_t=None):
    bsz, total, d = h.shape
    n = w.shape[1]
    rows = TOKEN_TILE
    in_specs = [pl.BlockSpec((1, rows, d), lambda b, i: (b, i, 0)),
                _const_spec((1, d)),
                pl.BlockSpec((1, 2, 6, d), lambda b, i: (b, 0, 0, 0)),
                _const_spec((d, n))]
    out_specs = [pl.BlockSpec((1, rows, n), lambda b, i: (b, i, 0))]
    out_shape = [jax.ShapeDtypeStruct((bsz, total, n), F32)]
    args = [h, g.reshape(1, d), modtab, w]
    if w_t is not None:
        nt = w_t.shape[0]
        in_specs.append(_const_spec((nt, d)))
        out_specs.append(pl.BlockSpec((1, nt, rows), lambda b, i: (b, 0, i)))
        out_shape.append(jax.ShapeDtypeStruct((bsz, nt, total), F32))
        args.append(w_t)
    out = pl.pallas_call(
        functools.partial(_inproj_kernel, rows=rows, ctx_len=ctx_len, with_t=w_t is not None),
        grid=(bsz, total // rows),
        in_specs=in_specs, out_specs=out_specs, out_shape=out_shape,
        compiler_params=_params(),
        name="in_projection",
    )(*args)
    return out if w_t is not None else out[0]


def _retention_kernel(p_ref, cos_ref, sin_ref, lg_ref, *rest, reverse):
    if reverse:
        yf_ref, o_ref, s_ref = rest
    else:
        o_ref, s_ref = rest
    c = RET_CHUNK

    @pl.when(pl.program_id(1) == 0)
    def _():
        s_ref[...] = jnp.zeros_like(s_ref)

    lg = lg_ref[...]
    pos = _iota2((c, 1), 0).astype(F32)
    q_exp = (c - pos) if reverse else (pos + 1.0)
    k_exp = pos if reverse else (c - 1.0 - pos)
    ii = _iota2((c, c), 0)
    jj = _iota2((c, c), 1)
    rel = (jj - ii) if reverse else (ii - jj)
    keep = rel >= 0
    relf = jnp.where(keep, rel, 0).astype(F32)
    cos = cos_ref[...]
    sin = sin_ref[...]
    w = N_HEADS * HEAD_DK
    for h in range(N_HEADS):
        sl = slice(h * HEAD_DK, (h + 1) * HEAD_DK)
        lgh = lg[:, sl]
        q = p_ref[0, :, sl]
        k = p_ref[0, :, w + h * HEAD_DK:w + (h + 1) * HEAD_DK] * (HEAD_DK ** -0.5)
        v = p_ref[0, :, 2 * w + h * HEAD_DK:2 * w + (h + 1) * HEAD_DK]
        q = q * cos + pltpu.roll(q, HEAD_DK // 2, 1) * sin
        k = k * cos + pltpu.roll(k, HEAD_DK // 2, 1) * sin
        dmat = jnp.where(keep, jnp.exp(relf * lgh), 0.0)
        scores = _mm_nt(q, k) * dmat
        state = s_ref[h]
        y = _mm(scores, v) + _mm(q * jnp.exp(q_exp * lgh), state)
        s_ref[h] = state * jnp.exp(c * lgh) + _mm_tn(k * jnp.exp(k_exp * lgh), v)
        if reverse:
            y = y + yf_ref[0, :, sl]
            g = p_ref[0, :, 3 * w + h * HEAD_DK:3 * w + (h + 1) * HEAD_DK]
            y = y * lax.rsqrt(jnp.mean(y * y, axis=-1, keepdims=True) + EPS) * _silu(g)
        o_ref[0, :, sl] = y


def _retention(p, cos, sin, log_gamma_rows, ctx_len):
    bsz, total, _ = p.shape
    w = N_HEADS * HEAD_DK
    c = RET_CHUNK
    n_all, n_ctx = total // c, ctx_len // c
    y = None
    for reverse in (False, True):
        def blk(b, s, reverse=reverse):
            return (b, _scan_block(s, n_ctx, n_all, reverse), 0)

        def tab(b, s, reverse=reverse):
            return (_scan_block(s, n_ctx, n_all, reverse), 0)

        in_specs = [pl.BlockSpec((1, c, 4 * w), blk),
                    pl.BlockSpec((c, HEAD_DK), tab),
                    pl.BlockSpec((c, HEAD_DK), tab),
                    _const_spec((1, w))]
        args = [p, cos, sin, log_gamma_rows[int(reverse)]]
        if reverse:
            in_specs.append(pl.BlockSpec((1, c, w), blk))
            args.append(y)
        y = pl.pallas_call(
            functools.partial(_retention_kernel, reverse=reverse),
            grid=(bsz, n_all),
            in_specs=in_specs,
            out_specs=pl.BlockSpec((1, c, w), blk),
            out_shape=jax.ShapeDtypeStruct((bsz, total, w), F32),
            scratch_shapes=[pltpu.VMEM((N_HEADS, HEAD_DK, HEAD_DK), F32)],
            compiler_params=_params(),
            name="retention_bwd" if reverse else "retention_fwd",
        )(*args)
    return y


def _lru_kernel(x_ref, xp_ref, xn_ref, cw_ref, cb_ref, gw_ref, gb_ref, lam_ref, *rest,
                reverse, ctx_len, total, n_ctx, n_all):
    if reverse:
        hf_ref, o_ref, buf_ref, carry_ref = rest
    else:
        o_ref, buf_ref, carry_ref = rest
    rows = LRU_BLOCK_ROWS
    s = pl.program_id(1)
    blk = _scan_block(s, n_ctx, n_all, reverse)

    @pl.when(s == 0)
    def _():
        carry_ref[...] = jnp.zeros_like(carry_ref)

    w = x_ref.shape[2] // 2
    _fill_halo(buf_ref, x_ref[0, :, 0:w], xp_ref[0], xn_ref[0], blk * rows, rows, ctx_len, total)
    xc = cb_ref[...] + sum(cw_ref[j:j + 1, :] * _shifted(buf_ref, j - CONV_K // 2, rows) for j in range(CONV_K))
    n_blocks = gw_ref.shape[0]
    bw = w // n_blocks
    parts = [_mm(xc[:, n * bw:(n + 1) * bw], gw_ref[n]) for n in range(n_blocks)]
    r_gate = _sigmoid(jnp.concatenate([p[:, :bw] for p in parts], axis=1) + gb_ref[0:1, :])
    i_gate = _sigmoid(jnp.concatenate([p[:, bw:] for p in parts], axis=1) + gb_ref[1:2, :])
    log_a = LRU_C * r_gate * (-_softplus(-lam_ref[...]))
    a = jnp.exp(log_a)
    b = jnp.sqrt(1.0 - jnp.exp(2.0 * log_a)) * i_gate * xc
    ridx = _iota2((rows, 1), 0)
    d = 1
    while d < rows:
        if reverse:
            a_s, b_s, ok = pltpu.roll(a, rows - d, 0), pltpu.roll(b, rows - d, 0), ridx < rows - d
        else:
            a_s, b_s, ok = pltpu.roll(a, d, 0), pltpu.roll(b, d, 0), ridx >= d
        b = b + a * jnp.where(ok, b_s, 0.0)
        a = a * jnp.where(ok, a_s, 1.0)
        d *= 2
    h = b + a * carry_ref[...]
    carry_ref[...] = h[0:1, :] if reverse else h[rows - 1:rows, :]
    if reverse:
        h = (h + hf_ref[0]) * _gelu_tanh(x_ref[0, :, w:2 * w])
    o_ref[0] = h


def _rglru(p, conv_w, conv_b, gate_w, gate_b, lam, ctx_len):
    bsz, total, w2 = p.shape
    w = w2 // 2
    rows = LRU_BLOCK_ROWS
    n_all, n_ctx = total // rows, ctx_len // rows
    per = rows // HALO
    n8 = total // HALO
    n_blocks, bw = gate_w.shape[2], gate_w.shape[3]
    gw = jnp.transpose(gate_w, (0, 2, 3, 1, 4)).reshape(N_DIR, n_blocks, bw, 2 * bw).astype(BF16)
    h = None
    for reverse in (False, True):
        def blk(b, s, reverse=reverse):
            return (b, _scan_block(s, n_ctx, n_all, reverse), 0)

        def prev(b, s, reverse=reverse):
            return (b, jnp.maximum(_scan_block(s, n_ctx, n_all, reverse) * per - 1, 0), 0)

        def nxt(b, s, reverse=reverse):
            return (b, jnp.minimum((_scan_block(s, n_ctx, n_all, reverse) + 1) * per, n8 - 1), 0)

        d = int(reverse)
        in_specs = [pl.BlockSpec((1, rows, w2), blk),
                    pl.BlockSpec((1, HALO, w), prev),
                    pl.BlockSpec((1, HALO, w), nxt),
                    _const_spec((CONV_K, w)), _const_spec((1, w)),
                    _const_spec((n_blocks, bw, 2 * bw)), _const_spec((2, w)), _const_spec((1, w))]
        args = [p, p, p, conv_w, conv_b.reshape(1, w), gw[d], gate_b[d], lam[d].reshape(1, w)]
        if reverse:
            in_specs.append(pl.BlockSpec((1, rows, w), blk))
            args.append(h)
        h = pl.pallas_call(
            functools.partial(_lru_kernel, reverse=reverse, ctx_len=ctx_len, total=total,
                              n_ctx=n_ctx, n_all=n_all),
            grid=(bsz, n_all),
            in_specs=in_specs,
            out_specs=pl.BlockSpec((1, rows, w), blk),
            out_shape=jax.ShapeDtypeStruct((bsz, total, w), F32),
            scratch_shapes=[pltpu.VMEM((rows + 2 * HALO, w), F32), pltpu.VMEM((1, w), F32)],
            compiler_params=_params(),
            name="rglru_bwd" if reverse else "rglru_fwd",
        )(*args)
    return h


def _gdn_kernel(x_ref, xp_ref, xn_ref, abt_ref, cw_ref, alog_r_ref, dtb_r_ref, alog_c_ref, dtb_c_ref,
                ng_ref, *rest, reverse, ctx_len, total, n_ctx, n_all):
    if reverse:
        yf_ref, o_ref, buf_ref, s_ref = rest
    else:
        o_ref, buf_ref, s_ref = rest
    c = GDN_CHUNK
    s = pl.program_id(0)
    blk = _scan_block(s, n_ctx, n_all, reverse)
    bsz = x_ref.shape[0]

    @pl.when(s == 0)
    def _():
        s_ref[...] = jnp.zeros_like(s_ref)

    w = N_HEADS * HEAD_DK
    ii = _iota2((c, c), 0)
    jj = _iota2((c, c), 1)
    incl = (ii <= jj) if reverse else (ii >= jj)
    strict = (ii < jj) if reverse else (ii > jj)
    incl_t = (ii >= jj) if reverse else (ii <= jj)
    last = 0 if reverse else c - 1

    chains = [(b, h) for b in range(bsz) for h in range(N_HEADS)]
    states = [s_ref[b * N_HEADS + h] for b, h in chains]
    qs, ks, vbs, kbes, decays, e_gcs, g_lasts, gccs = [], [], [], [], [], [], [], []
    for b in range(bsz):
        _fill_halo(buf_ref.at[b], x_ref[b, :, 0:3 * w], xp_ref[b], xn_ref[b], blk * c, c, ctx_len, total)
        qkv = _silu(sum(cw_ref[j:j + 1, :] * _shifted(buf_ref.at[b], j - CONV_K // 2, c) for j in range(CONV_K)))
        ab = x_ref[b, :, 4 * w:4 * w + LANES]
        abt = abt_ref[b]
        g_col = -jnp.exp(alog_r_ref[...]) * _softplus(ab + dtb_r_ref[...])
        g_row = -jnp.exp(alog_c_ref[...]) * _softplus(abt + dtb_c_ref[...])
        beta_col = _sigmoid(ab)
        gc_col = _tri_mm(jnp.where(incl, 1.0, 0.0), g_col)
        gc_row = _mm_tri(g_row, jnp.where(incl_t, 1.0, 0.0))
        for h in range(N_HEADS):
            col = int(reverse) * N_HEADS + h
            gcc = gc_col[:, col:col + 1]
            gcr = gc_row[col:col + 1, :]
            beta = beta_col[:, 2 * N_HEADS + col:2 * N_HEADS + col + 1]
            decays.append(jnp.where(incl, jnp.exp(jnp.where(incl, gcc - gcr, 0.0)), 0.0))
            q = qkv[:, h * HEAD_DK:(h + 1) * HEAD_DK]
            k = qkv[:, w + h * HEAD_DK:w + (h + 1) * HEAD_DK]
            v = qkv[:, 2 * w + h * HEAD_DK:2 * w + (h + 1) * HEAD_DK]
            qs.append(q * lax.rsqrt(jnp.sum(q * q, axis=-1, keepdims=True) + EPS) * (HEAD_DK ** -0.5))
            k = k * lax.rsqrt(jnp.sum(k * k, axis=-1, keepdims=True) + EPS)
            ks.append(k)
            kbes.append(k * beta)
            vbs.append(v * beta)
            gccs.append(gcc)
            e_gcs.append(jnp.exp(gcc))
            g_lasts.append(gcc[last:last + 1, :])

    kq = [_mm_nt(jnp.concatenate([kb, q], axis=0), k) for kb, q, k in zip(kbes, qs, ks)]
    t_invs = _tri_inverse([jnp.where(strict, m[0:c] * dec, 0.0) for m, dec in zip(kq, decays)], c)
    qks = [m[c:2 * c] * dec for m, dec in zip(kq, decays)]
    uw = [_mm(t, jnp.concatenate([vb, kb * e], axis=1)) for t, vb, kb, e in zip(t_invs, vbs, kbes, e_gcs)]
    ws = [_mm(jnp.concatenate([m[:, HEAD_DK:], q * e], axis=0), st)
          for m, q, e, st in zip(uw, qs, e_gcs, states)]
    v_news = [m[:, :HEAD_DK] - x[0:c] for m, x in zip(uw, ws)]
    ys = [x[c:2 * c] + _mm(qk, vn) for x, qk, vn in zip(ws, qks, v_news)]
    new_states = [st * jnp.exp(gl) + _mm_tn(k * jnp.exp(gl - gcc), vn)
                  for st, gl, k, gcc, vn in zip(states, g_lasts, ks, gccs, v_news)]
    for i, (b, h) in enumerate(chains):
        s_ref[b * N_HEADS + h] = new_states[i]
        y = ys[i]
        if reverse:
            y = y + yf_ref[b, :, h * HEAD_DK:(h + 1) * HEAD_DK]
            z = x_ref[b, :, 3 * w + h * HEAD_DK:3 * w + (h + 1) * HEAD_DK]
            y = y * lax.rsqrt(jnp.mean(y * y, axis=-1, keepdims=True) + EPS) * ng_ref[...] * _silu(z)
        o_ref[b, :, h * HEAD_DK:(h + 1) * HEAD_DK] = y


def _gated_deltanet(p, abt, conv_w, a_log, dt_bias, norm_g, ctx_len):
    bsz, total, pw = p.shape
    w = N_HEADS * HEAD_DK
    c = GDN_CHUNK
    n_all, n_ctx = total // c, ctx_len // c
    per = c // HALO
    n8 = total // HALO
    nt = abt.shape[1]
    flat_alog = a_log.reshape(-1)
    flat_dtb = dt_bias.reshape(-1)
    alog_r = jnp.zeros((1, LANES), F32).at[0, :flat_alog.shape[0]].set(flat_alog)
    dtb_r = jnp.zeros((1, LANES), F32).at[0, :flat_dtb.shape[0]].set(flat_dtb)
    alog_c = jnp.zeros((nt, 1), F32).at[:flat_alog.shape[0], 0].set(flat_alog)
    dtb_c = jnp.zeros((nt, 1), F32).at[:flat_dtb.shape[0], 0].set(flat_dtb)
    y = None
    for reverse in (False, True):
        def blk(s, reverse=reverse):
            return (0, _scan_block(s, n_ctx, n_all, reverse), 0)

        def blk_t(s, reverse=reverse):
            return (0, 0, _scan_block(s, n_ctx, n_all, reverse))

        def prev(s, reverse=reverse):
            return (0, jnp.maximum(_scan_block(s, n_ctx, n_all, reverse) * per - 1, 0), 0)

        def nxt(s, reverse=reverse):
            return (0, jnp.minimum((_scan_block(s, n_ctx, n_all, reverse) + 1) * per, n8 - 1), 0)

        in_specs = [pl.BlockSpec((bsz, c, pw), blk),
                    pl.BlockSpec((bsz, HALO, 3 * w), prev),
                    pl.BlockSpec((bsz, HALO, 3 * w), nxt),
                    pl.BlockSpec((bsz, nt, c), blk_t),
                    _const_spec((CONV_K, 3 * w)),
                    _const_spec((1, LANES)), _const_spec((1, LANES)),
                    _const_spec((nt, 1)), _const_spec((nt, 1)),
                    _const_spec((1, HEAD_DK))]
        args = [p, p, p, abt, conv_w, alog_r, dtb_r, alog_c, dtb_c, norm_g.reshape(1, HEAD_DK)]
        if reverse:
            in_specs.append(pl.BlockSpec((bsz, c, w), blk))
            args.append(y)
        y = pl.pallas_call(
            functools.partial(_gdn_kernel, reverse=reverse, ctx_len=ctx_len, total=total,
                              n_ctx=n_ctx, n_all=n_all),
            grid=(n_all,),
            in_specs=in_specs,
            out_specs=pl.BlockSpec((bsz, c, w), blk),
            out_shape=jax.ShapeDtypeStruct((bsz, total, w), F32),
            scratch_shapes=[pltpu.VMEM((bsz, c + 2 * HALO, 3 * w), F32),
                            pltpu.VMEM((bsz * N_HEADS, HEAD_DK, HEAD_DK), F32)],
            compiler_params=_params(1),
            name="gdn_bwd" if reverse else "gdn_fwd",
        )(*args)
    return y


def _rwkv_kernel(x_ref, xp_ref, xn_ref, mu_ref, w0_ref, w2_ref, a0_ref, a2_ref, kk_ref, ka_ref, hs_ref, *rest,
                 reverse, ctx_len, total, n_ctx, n_all):
    if reverse:
        g2_ref, rk_ref, lng_ref, lnb_ref, yf_ref, o_ref, buf_ref, s_ref = rest
    else:
        o_ref, buf_ref, s_ref = rest
    c = RWKV_CHUNK
    step = pl.program_id(0)
    blk = _scan_block(step, n_ctx, n_all, reverse)
    bsz = x_ref.shape[0]
    n_pairs = s_ref.shape[0] // bsz
    bw = n_pairs * LANES

    @pl.when(step == 0)
    def _():
        s_ref[...] = jnp.zeros_like(s_ref)

    rows = []
    for b in range(bsz):
        xb = x_ref[b]
        _fill_halo(buf_ref.at[b], xb, xp_ref[b], xn_ref[b], blk * c, c, ctx_len, total)
        rows.append(xb + (0.5 * (_shifted(buf_ref.at[b], -1, c) + _shifted(buf_ref.at[b], 1, c)) - xb) * mu_ref[...])
    x = jnp.concatenate(rows, axis=0)
    n = bsz * c
    r = x[:, 0:bw]
    k = x[:, bw:2 * bw]
    v = x[:, 2 * bw:3 * bw]
    off = 3 * bw
    gd = x[:, off:off + RWKV_GATE_RANK]
    off += RWKV_GATE_RANK
    wd = [x[:, off + d * RWKV_RANK:off + (d + 1) * RWKV_RANK] for d in range(N_DIR)]
    off += N_DIR * RWKV_RANK
    ad = [x[:, off + d * RWKV_RANK:off + (d + 1) * RWKV_RANK] for d in range(N_DIR)]

    def per_head_sum(t):
        t1, t2, t3 = _split3(t)
        hs = hs_ref[...]
        return (jnp.dot(t1, hs, preferred_element_type=F32) + jnp.dot(t2, hs, preferred_element_type=F32)
                + jnp.dot(t3, hs, preferred_element_type=F32))

    def a_of(d):
        return _sigmoid(a0_ref[d:d + 1, :] + _mm(ad[d], a2_ref[d]))

    d = int(reverse)
    w_raw = -_softplus(-(w0_ref[d:d + 1, :] + _mm(jnp.tanh(wd[d]), w2_ref[d]))) - 0.5
    logw = -jnp.exp(w_raw)
    a_dir = a_of(d)
    kkv = k * kk_ref[...]
    kkv = kkv * lax.rsqrt(per_head_sum(kkv * kkv) + EPS)
    kd = k * (1.0 + (a_dir - 1.0) * ka_ref[...])

    cshift = c.bit_length() - 1

    def chunk_masks(size):
        ii = _iota2((size, size), 0)
        jj = _iota2((size, size), 1)
        same = (ii >> cshift) == (jj >> cshift)
        return ((ii <= jj) if reverse else (ii >= jj)) & same, ((ii < jj) if reverse else (ii > jj)) & same

    cum = _tri_mm(jnp.where(chunk_masks(n)[0], 1.0, 0.0), logw)
    last = 0 if reverse else c - 1
    cum_tot = jnp.concatenate([jnp.broadcast_to(cum[b * c + last:b * c + last + 1, :], (c, bw))
                               for b in range(bsz)], axis=0)
    w_inv = jnp.exp(-cum)
    w_rest = jnp.exp(cum_tot - cum)
    alpha_t = -kkv * jnp.exp(cum - logw)
    beta = kkv * a_dir
    beta_t = beta * w_inv
    k_t = kd * w_inv
    r_t = r * jnp.exp(cum)
    beta_h = beta * w_rest
    k_h = kd * w_rest

    n2 = 2 * c
    incl, strict = chunk_masks(n2)
    lane = _iota2((1, LANES), 1)
    m0 = jnp.where(lane < RWKV_N, 1.0, 0.0)
    m1 = 1.0 - m0
    incl2 = jnp.concatenate([incl, incl], axis=1)

    def stack(t, b, p):
        t = t[b * c:(b + 1) * c, p * LANES:(p + 1) * LANES]
        return jnp.concatenate([t * m0, t * m1], axis=0)

    chains = [(b, p) for b in range(bsz) for p in range(n_pairs)]
    states = [s_ref[b * n_pairs + p] for b, p in chains]
    ar = [jnp.concatenate([stack(alpha_t, b, p), stack(r_t, b, p)], axis=0) for b, p in chains]
    bk = [jnp.concatenate([stack(beta_t, b, p), stack(k_t, b, p)], axis=0) for b, p in chains]
    vs = [stack(v, b, p) for b, p in chains]
    bkh = [jnp.concatenate([stack(beta_h, b, p), stack(k_h, b, p)], axis=0) for b, p in chains]
    g = [_mm_nt(x1, x2) for x1, x2 in zip(ar, bk)]
    t_invs = _tri_inverse([jnp.where(strict, -m[0:n2, 0:n2], 0.0) for m in g], c)
    mv = [_mm(jnp.where(strict, m[0:n2, n2:2 * n2], 0.0), vv) for m, vv in zip(g, vs)]
    xs = [_mm_nt(x1, st) for x1, st in zip(ar, states)]
    us = [_mm(t, x1[0:n2] + m) for t, x1, m in zip(t_invs, xs, mv)]
    uv = [jnp.concatenate([u, vv], axis=0) for u, vv in zip(us, vs)]
    ys = [x1[n2:2 * n2] + _mm(jnp.where(incl2, m[n2:2 * n2, :], 0.0), w) for x1, m, w in zip(xs, g, uv)]
    for i, (b, p) in enumerate(chains):
        w_tot = jnp.exp(cum_tot[b * c:b * c + 1, p * LANES:(p + 1) * LANES])
        s_ref[b * n_pairs + p] = states[i] * w_tot + _mm_tn(uv[i], bkh[i])
    y = jnp.concatenate([jnp.concatenate([ys[b * n_pairs + p][0:c] + ys[b * n_pairs + p][c:n2]
                                          for p in range(n_pairs)], axis=1) for b in range(bsz)], axis=0)
    if reverse:
        y = y + jnp.concatenate([yf_ref[b] for b in range(bsz)], axis=0)
        mean = per_head_sum(y) * (1.0 / RWKV_N)
        yc = y - mean
        var = per_head_sum(yc * yc) * (1.0 / RWKV_N)
        yn = yc * lax.rsqrt(var + RWKV_LN_EPS) * lng_ref[...] + lnb_ref[...]
        k_sum = kd + k * (1.0 + (a_of(0) - 1.0) * ka_ref[...])
        bonus = per_head_sum(r * k_sum * rk_ref[...]) * v
        y = (yn + bonus) * _mm(_sigmoid(gd), g2_ref[...])
    for b in range(bsz):
        o_ref[b] = y[b * c:(b + 1) * c]


def _rwkv7(p, mu, w0, w2, a0, a2, g2, k_k, k_a, r_k, ln_g, ln_b, ctx_len):
    bsz, total, pw = p.shape
    bw = k_k.shape[0]
    c = RWKV_CHUNK
    n_all, n_ctx = total // c, ctx_len // c
    per = c // HALO
    n8 = total // HALO
    head_id = jnp.arange(bw) // RWKV_N
    head_sum = (head_id[:, None] == head_id[None, :]).astype(BF16)
    y = None
    for reverse in (False, True):
        def blk(s, reverse=reverse):
            return (0, _scan_block(s, n_ctx, n_all, reverse), 0)

        def prev(s, reverse=reverse):
            return (0, jnp.maximum(_scan_block(s, n_ctx, n_all, reverse) * per - 1, 0), 0)

        def nxt(s, reverse=reverse):
            return (0, jnp.minimum((_scan_block(s, n_ctx, n_all, reverse) + 1) * per, n8 - 1), 0)

        in_specs = [pl.BlockSpec((bsz, c, pw), blk),
                    pl.BlockSpec((bsz, HALO, pw), prev),
                    pl.BlockSpec((bsz, HALO, pw), nxt),
                    _const_spec((1, pw)),
                    _const_spec((N_DIR, bw)), _const_spec((N_DIR, RWKV_RANK, bw)),
                    _const_spec((N_DIR, bw)), _const_spec((N_DIR, RWKV_RANK, bw)),
                    _const_spec((1, bw)), _const_spec((1, bw)), _const_spec((bw, bw))]
        args = [p, p, p, mu.reshape(1, pw), w0, w2.astype(BF16), a0, a2.astype(BF16),
                k_k.reshape(1, bw), k_a.reshape(1, bw), head_sum]
        if reverse:
            in_specs += [_const_spec((RWKV_GATE_RANK, bw)), _const_spec((1, bw)), _const_spec((1, bw)),
                         _const_spec((1, bw)), pl.BlockSpec((bsz, c, bw), blk)]
            args += [g2.astype(BF16), r_k.reshape(1, bw), ln_g.reshape(1, bw), ln_b.reshape(1, bw), y]
        y = pl.pallas_call(
            functools.partial(_rwkv_kernel, reverse=reverse, ctx_len=ctx_len, total=total,
                              n_ctx=n_ctx, n_all=n_all),
            grid=(n_all,),
            in_specs=in_specs,
            out_specs=pl.BlockSpec((bsz, c, bw), blk),
            out_shape=jax.ShapeDtypeStruct((bsz, total, bw), F32),
            scratch_shapes=[pltpu.VMEM((bsz, c + 2 * HALO, pw), F32),
                            pltpu.VMEM((bsz * (bw // LANES), LANES, LANES), F32)],
            compiler_params=_params(1),
            name="rwkv_bwd" if reverse else "rwkv_fwd",
        )(*args)
    return y


def _merge_kernel(h_ref, g_ref, mod_ref, wg_ref, gb_ref, y0_ref, y1_ref, y2_ref, y3_ref, bw_ref, ow_ref,
                  o_ref, *, rows, ctx_len):
    tile = pl.program_id(1)
    h = h_ref[0]
    d = h.shape[1]
    xm = _norm_modulate(h, g_ref[...], mod_ref, tile, rows, ctx_len, 0, 1).astype(BF16)
    merged = jnp.zeros_like(h)
    for k, y_ref in enumerate((y0_ref, y1_ref, y2_ref, y3_ref)):
        gate = _sigmoid(jnp.dot(xm, wg_ref[:, k * d:(k + 1) * d], preferred_element_type=F32)
                        + gb_ref[:, k * d:(k + 1) * d])
        merged = merged + gate * _mm(y_ref[0], bw_ref[k])
    o_ref[0] = h + _mod_row(mod_ref, tile, rows, ctx_len, 2) * _mm(merged, ow_ref[...])


def _merge(h, g, modtab, w_gates, gate_b, ys, branch_w, out_w, ctx_len):
    bsz, total, d = h.shape
    rows = TOKEN_TILE
    bwid = ys[0].shape[2]
    row_spec = lambda width: pl.BlockSpec((1, rows, width), lambda b, i: (b, i, 0))
    return pl.pallas_call(
        functools.partial(_merge_kernel, rows=rows, ctx_len=ctx_len),
        grid=(bsz, total // rows),
        in_specs=[row_spec(d), _const_spec((1, d)),
                  pl.BlockSpec((1, 2, 6, d), lambda b, i: (b, 0, 0, 0)),
                  _const_spec((d, N_BRANCH * d)), _const_spec((1, N_BRANCH * d)),
                  row_spec(bwid), row_spec(bwid), row_spec(bwid), row_spec(bwid),
                  _const_spec((N_BRANCH, bwid, d)), _const_spec((d, d))],
        out_specs=row_spec(d),
        out_shape=jax.ShapeDtypeStruct((bsz, total, d), F32),
        compiler_params=_params(),
        name="merge",
    )(h, g.reshape(1, d), modtab, w_gates, gate_b.reshape(1, -1), *ys, branch_w, out_w)


def _ffn_kernel(h_ref, g_ref, mod_ref, w1_ref, w3_ref, w2_ref, fg_ref, o_ref, *, rows, ctx_len, final):
    tile = pl.program_id(1)
    h = h_ref[0]
    u = _norm_modulate(h, g_ref[...], mod_ref, tile, rows, ctx_len, 3, 4).astype(BF16)
    t = _silu(jnp.dot(u, w1_ref[...], preferred_element_type=F32)) * jnp.dot(u, w3_ref[...],
                                                                             preferred_element_type=F32)
    h = h + _mod_row(mod_ref, tile, rows, ctx_len, 5) * _mm(t, w2_ref[...])
    if final:
        h = h * lax.rsqrt(jnp.mean(h * h, axis=-1, keepdims=True) + EPS) * fg_ref[...]
    o_ref[0] = h


def _ffn(h, g, modtab, w1, w3, w2, final_g, ctx_len, final):
    bsz, total, d = h.shape
    rows = TOKEN_TILE
    dff = w1.shape[1]
    row_spec = pl.BlockSpec((1, rows, d), lambda b, i: (b, i, 0))
    return pl.pallas_call(
        functools.partial(_ffn_kernel, rows=rows, ctx_len=ctx_len, final=final),
        grid=(bsz, total // rows),
        in_specs=[row_spec, _const_spec((1, d)),
                  pl.BlockSpec((1, 2, 6, d), lambda b, i: (b, 0, 0, 0)),
                  _const_spec((d, dff)), _const_spec((d, dff)), _const_spec((dff, d)), _const_spec((1, d))],
        out_specs=row_spec,
        out_shape=jax.ShapeDtypeStruct((bsz, total, d), F32),
        compiler_params=_params(),
        name="ffn",
    )(h, g.reshape(1, d), modtab, w1, w3, w2, final_g.reshape(1, d))


def _rope_tables(seq, ctx_len):
    t = jnp.arange(seq)
    row = (t // GRID_W).astype(F32)
    col = (t % GRID_W).astype(F32)
    quarter = HEAD_DK // 4
    inv = ROPE_BASE ** (-jnp.arange(quarter, dtype=F32) / quarter)
    ang = jnp.concatenate([row[:, None] * inv, col[:, None] * inv], axis=-1)
    cos, sin = jnp.cos(ang), jnp.sin(ang)
    cos = jnp.concatenate([cos, cos], axis=-1)
    sin = jnp.concatenate([-sin, sin], axis=-1)
    cos = jnp.concatenate([jnp.ones((ctx_len, HEAD_DK), F32), cos], axis=0)
    sin = jnp.concatenate([jnp.zeros((ctx_len, HEAD_DK), F32), sin], axis=0)
    return cos, sin


def kernel(x, c, ctx, c_ctx, mod_w, mod_b, norm1_g, norm2_g, in_w, gate_b, ret_decay_exp, lru_conv_w,
           lru_conv_b, lru_gate_w, lru_gate_b, lru_lambda, gdn_conv_w, gdn_a_log, gdn_dt_bias, gdn_norm_g,
           rwkv_mu, rwkv_w0, rwkv_w2, rwkv_a0, rwkv_a2, rwkv_g2, rwkv_k_k, rwkv_k_a, rwkv_r_k, rwkv_ln_g,
           rwkv_ln_b, branch_w, out_w, ffn_w1, ffn_w3, ffn_w2, final_norm_g):
    bsz, seq, d = x.shape
    ctx_len = ctx.shape[1]
    depth = mod_w.shape[0]
    bw = d // 2
    assert ctx_len % LRU_BLOCK_ROWS == 0 and seq % LRU_BLOCK_ROWS == 0 and bsz + 1 <= HALO

    h = jnp.concatenate([ctx, x], axis=1)
    cond = jnp.zeros((HALO, d), F32).at[:bsz].set(c).at[bsz].set(c_ctx)
    mods = _modulation(cond, mod_w, mod_b).reshape(depth, HALO, 6, d)
    cos, sin = _rope_tables(seq, ctx_len)

    widths = (4 * bw, 2 * bw, 3 * bw, bw, N_DIR * N_HEADS, N_DIR * N_HEADS,
              3 * bw + RWKV_GATE_RANK + 2 * N_DIR * RWKV_RANK, N_BRANCH * d)
    offs = [0]
    for wdt in widths:
        offs.append(offs[-1] + wdt)

    for l in range(depth):
        lat = mods[l, :bsz]
        modtab = jnp.stack([jnp.broadcast_to(mods[l, bsz], lat.shape), lat], axis=1)
        wl = in_w[l]
        w_ret = wl[:, offs[0]:offs[1]].astype(BF16)
        w_lru = wl[:, offs[1]:offs[2]].astype(BF16)
        w_ab = wl[:, offs[4]:offs[6]]
        w_gdn = jnp.concatenate([wl[:, offs[2]:offs[4]], w_ab,
                                 jnp.zeros((d, LANES - w_ab.shape[1]), F32)], axis=1).astype(BF16)
        w_abt = w_ab.T.astype(BF16)
        w_rwkv = wl[:, offs[6]:offs[7]].astype(BF16)
        w_gates = wl[:, offs[7]:offs[8]].astype(BF16)

        p_ret = _in_projection(h, norm1_g[l], modtab, w_ret, ctx_len)
        p_lru = _in_projection(h, norm1_g[l], modtab, w_lru, ctx_len)
        p_gdn, abt = _in_projection(h, norm1_g[l], modtab, w_gdn, ctx_len, w_t=w_abt)
        p_rwkv = _in_projection(h, norm1_g[l], modtab, w_rwkv, ctx_len)

        log_gamma = jnp.log1p(-jnp.exp2(-ret_decay_exp[l].astype(F32)))
        lg_rows = jnp.repeat(log_gamma, HEAD_DK, axis=1).reshape(N_DIR, 1, N_HEADS * HEAD_DK)
        y_ret = _retention(p_ret, cos, sin, lg_rows, ctx_len)
        y_lru = _rglru(p_lru, lru_conv_w[l], lru_conv_b[l], lru_gate_w[l], lru_gate_b[l], lru_lambda[l], ctx_len)
        y_gdn = _gated_deltanet(p_gdn, abt, gdn_conv_w[l], gdn_a_log[l], gdn_dt_bias[l], gdn_norm_g[l], ctx_len)
        y_rwkv = _rwkv7(p_rwkv, rwkv_mu[l], rwkv_w0[l], rwkv_w2[l], rwkv_a0[l], rwkv_a2[l], rwkv_g2[l],
                        rwkv_k_k[l], rwkv_k_a[l], rwkv_r_k[l].reshape(-1), rwkv_ln_g[l], rwkv_ln_b[l], ctx_len)

        h = _merge(h, norm1_g[l], modtab, w_gates, gate_b[l], (y_ret, y_lru, y_gdn, y_rwkv),
                   branch_w[l].astype(BF16), out_w[l].astype(BF16), ctx_len)
        h = _ffn(h, norm2_g[l], modtab, ffn_w1[l].astype(BF16), ffn_w3[l].astype(BF16),
                 ffn_w2[l].astype(BF16), final_norm_g, ctx_len, final=(l == depth - 1))
    return h[:, ctx_len:]
```

```python
import functools

import jax
import jax.numpy as jnp
from jax import lax
from jax.experimental import pallas as pl
from jax.experimental.pallas import tpu as pltpu

F32 = jnp.float32
BF16 = jnp.bfloat16

GRID_W = 64
ROPE_BASE = 10000.0
EPS = 1e-6
RWKV_LN_EPS = 64e-5
LRU_C = 8.0
N_DIR = 2
N_BRANCH = 4
HEAD_DK = 128
N_HEADS = 4
RWKV_N = 64
RWKV_RANK = 64
RWKV_GATE_RANK = 128
CONV_K = 4
LANES = 128
HALO = 8
VMEM_LIMIT_BYTES = 56 * 1024 * 1024

TOKEN_TILES = (768, 512, 256)
PROJ_TILES = (1408,) + TOKEN_TILES
RET_CHUNK = 128
LRU_BLOCK_ROWS = 256
GDN_CHUNK = 128
RWKV_CHUNK = 64
FFN_SPLIT = 2


def _mm(a, b):
    return jnp.dot(a.astype(BF16), b.astype(BF16), preferred_element_type=F32)


def _mm_nt(a, b):
    return lax.dot_general(a.astype(BF16), b.astype(BF16), (((1,), (1,)), ((), ())),
                           preferred_element_type=F32)


def _mm_tn(a, b):
    return lax.dot_general(a.astype(BF16), b.astype(BF16), (((0,), (0,)), ((), ())),
                           preferred_element_type=F32)


def _split3(x):
    x1 = x.astype(BF16)
    r1 = x - x1.astype(F32)
    x2 = r1.astype(BF16)
    x3 = (r1 - x2.astype(F32)).astype(BF16)
    return x1, x2, x3


def _tri_mm(tri, x):
    x1, x2, x3 = _split3(x)
    t = tri.astype(BF16)
    return (jnp.dot(t, x1, preferred_element_type=F32) + jnp.dot(t, x2, preferred_element_type=F32)
            + jnp.dot(t, x3, preferred_element_type=F32))


def _mm_tri(x, tri):
    x1, x2, x3 = _split3(x)
    t = tri.astype(BF16)
    return (jnp.dot(x1, t, preferred_element_type=F32) + jnp.dot(x2, t, preferred_element_type=F32)
            + jnp.dot(x3, t, preferred_element_type=F32))


def _sigmoid(x):
    return 1.0 / (1.0 + jnp.exp(-x))


def _silu(x):
    return x * _sigmoid(x)


def _softplus(x):
    return jnp.maximum(x, 0.0) + jnp.log1p(jnp.exp(-jnp.abs(x)))


def _gelu_tanh(x):
    return 0.5 * x * (1.0 + jnp.tanh(0.7978845608028654 * (x + 0.044715 * x * x * x)))


def _iota2(shape, axis):
    return lax.broadcasted_iota(jnp.int32, shape, axis)


def _scan_block(s, n_ctx, n_all, reverse):
    if not reverse:
        return s
    return jnp.where(s < n_ctx, n_ctx - 1 - s, n_all - 1 - (s - n_ctx))


def _tri_inverse(mats, n_max):
    n = mats[0].shape[0]
    ii = _iota2((n, n), 0)
    jj = _iota2((n, n), 1)
    eye = jnp.where(ii == jj, 1.0, 0.0)
    pair = (ii >> 1) == (jj >> 1)
    ts = [eye - jnp.where(pair, a, 0.0) for a in mats]
    b = 2
    while b < n_max:
        sh = b.bit_length() - 1
        between = ((ii >> (sh + 1)) == (jj >> (sh + 1))) & ((ii >> sh) != (jj >> sh))
        xs = [_mm(t, jnp.where(between, a, 0.0)) for t, a in zip(ts, mats)]
        ts = [t - _mm(x, t) for t, x in zip(ts, xs)]
        b *= 2
    return ts


def _fill_halo(buf_ref, x, prev, nxt, row0, rows, ctx_len, total):
    prev_ok = jnp.logical_and(row0 != 0, row0 != ctx_len)
    next_ok = jnp.logical_and(row0 + rows != ctx_len, row0 + rows != total)
    buf_ref[0:HALO, :] = jnp.where(prev_ok, prev, 0.0)
    buf_ref[HALO:HALO + rows, :] = x
    buf_ref[HALO + rows:HALO + rows + HALO, :] = jnp.where(next_ok, nxt, 0.0)


def _shifted(buf_ref, off, rows):
    return buf_ref[HALO + off:HALO + off + rows, :]


def _norm_modulate(x, g, mod_ref, tile, rows, ctx_len, shift_idx, scale_idx):
    xn = x * lax.rsqrt(jnp.mean(x * x, axis=-1, keepdims=True) + EPS) * g
    is_ctx = (tile * rows + _iota2((rows, 1), 0)) < ctx_len
    shift = jnp.where(is_ctx, mod_ref[0, 0, shift_idx:shift_idx + 1, :], mod_ref[0, 1, shift_idx:shift_idx + 1, :])
    scale = jnp.where(is_ctx, mod_ref[0, 0, scale_idx:scale_idx + 1, :], mod_ref[0, 1, scale_idx:scale_idx + 1, :])
    return xn * (1.0 + scale) + shift


def _mod_row(mod_ref, tile, rows, ctx_len, idx):
    is_ctx = (tile * rows + _iota2((rows, 1), 0)) < ctx_len
    return jnp.where(is_ctx, mod_ref[0, 0, idx:idx + 1, :], mod_ref[0, 1, idx:idx + 1, :])


def _params(n_axes=2):
    return pltpu.CompilerParams(dimension_semantics=("arbitrary",) * n_axes,
                                vmem_limit_bytes=VMEM_LIMIT_BYTES)


def _const_spec(shape):
    nd = len(shape)
    return pl.BlockSpec(shape, lambda *_: (0,) * nd)


def _weight_spec(shape):
    nd = len(shape)
    return pl.BlockSpec(shape, lambda *_: (0,) * nd, pipeline_mode=pl.Buffered(1))


def _token_tile(total, candidates):
    for rows in candidates:
        if total % rows == 0:
            return rows
    raise ValueError(f"token count {total} is not a multiple of {candidates[-1]}")


def _mod_kernel(c_ref, w_ref, b_ref, o_ref):
    o_ref[0] = _mm(_silu(c_ref[...]), w_ref[0]) + b_ref[0]


def _modulation(cond, mod_w, mod_b):
    depth, d, n = mod_w.shape
    tn = n // 4
    return pl.pallas_call(
        _mod_kernel,
        grid=(depth, n // tn),
        in_specs=[pl.BlockSpec(cond.shape, lambda l, j: (0, 0)),
                  pl.BlockSpec((1, d, tn), lambda l, j: (l, 0, j)),
                  pl.BlockSpec((1, 1, tn), lambda l, j: (l, 0, j))],
        out_specs=pl.BlockSpec((1, cond.shape[0], tn), lambda l, j: (l, 0, j)),
        out_shape=jax.ShapeDtypeStruct((depth, cond.shape[0], n), F32),
        compiler_params=_params(),
        name="modulation",
    )(cond, mod_w, mod_b.reshape(depth, 1, n))


def _normmod_kernel(h_ref, g_ref, mod_ref, o_ref, *, rows, ctx_len):
    o_ref[0] = _norm_modulate(h_ref[0], g_ref[...], mod_ref, pl.program_id(1), rows, ctx_len, 0, 1).astype(BF16)


def _mixer_input(h, g, modtab, ctx_len):
    bsz, total, d = h.shape
    rows = _token_tile(total, TOKEN_TILES)
    row_spec = pl.BlockSpec((1, rows, d), lambda b, i: (b, i, 0))
    return pl.pallas_call(
        functools.partial(_normmod_kernel, rows=rows, ctx_len=ctx_len),
        grid=(bsz, total // rows),
        in_specs=[row_spec, _const_spec((1, d)), pl.BlockSpec((1, 2, 6, d), lambda b, i: (b, 0, 0, 0))],
        out_specs=row_spec,
        out_shape=jax.ShapeDtypeStruct((bsz, total, d), BF16),
        compiler_params=_params(),
        name="norm_modulate",
    )(h, g.reshape(1, d), modtab)


def _inproj_kernel(x_ref, w_ref, *rest, with_t):
    if with_t:
        wt_ref, o_ref, ot_ref = rest
    else:
        (o_ref,) = rest
    o_ref[0] = jnp.dot(x_ref[0], w_ref[...], preferred_element_type=F32)
    if with_t:
        ot_ref[0] = lax.dot_general(wt_ref[...], x_ref[0], (((1,), (1,)), ((), ())),
                                    preferred_element_type=F32)


def _in_projection(xm, w, w_t=None):
    bsz, total, d = xm.shape
    n = w.shape[1]
    rows = _token_tile(total, PROJ_TILES)
    in_specs = [pl.BlockSpec((1, rows, d), lambda b, i: (b, i, 0)),
                _weight_spec((d, n))]
    out_specs = [pl.BlockSpec((1, rows, n), lambda b, i: (b, i, 0))]
    out_shape = [jax.ShapeDtypeStruct((bsz, total, n), F32)]
    args = [xm, w]
    if w_t is not None:
        nt = w_t.shape[0]
        in_specs.append(_const_spec((nt, d)))
        out_specs.append(pl.BlockSpec((1, nt, rows), lambda b, i: (b, 0, i)))
        out_shape.append(jax.ShapeDtypeStruct((bsz, nt, total), F32))
        args.append(w_t)
    out = pl.pallas_call(
        functools.partial(_inproj_kernel, with_t=w_t is not None),
        grid=(bsz, total // rows),
        in_specs=in_specs, out_specs=out_specs, out_shape=out_shape,
        compiler_params=_params(),
        name="in_projection",
    )(*args)
    return out if w_t is not None else out[0]


def _retention_kernel(p_ref, cos_ref, sin_ref, lg_ref, *rest, reverse):
    if reverse:
        yf_ref, o_ref, s_ref, dmat_ref, qdec_ref, kdec_ref = rest
    else:
        o_ref, s_ref, dmat_ref, qdec_ref, kdec_ref = rest
    c = RET_CHUNK
    bsz = p_ref.shape[0]
    lg = lg_ref[...]

    @pl.when(pl.program_id(0) == 0)
    def _():
        s_ref[...] = jnp.zeros_like(s_ref)
        pos = _iota2((c, 1), 0).astype(F32)
        qdec_ref[...] = jnp.exp(((c - pos) if reverse else (pos + 1.0)) * lg)
        kdec_ref[...] = jnp.exp((pos if reverse else (c - 1.0 - pos)) * lg)
        ii = _iota2((c, c), 0)
        jj = _iota2((c, c), 1)
        rel = (jj - ii) if reverse else (ii - jj)
        keep = rel >= 0
        relf = jnp.where(keep, rel, 0).astype(F32)
        for h in range(N_HEADS):
            dmat_ref[h] = jnp.where(keep, jnp.exp(relf * lg[:, h * HEAD_DK:(h + 1) * HEAD_DK]), 0.0)

    cos = cos_ref[...]
    sin = sin_ref[...]
    w = N_HEADS * HEAD_DK
    chains = [(b, h) for b in range(bsz) for h in range(N_HEADS)]
    states = [s_ref[b * N_HEADS + h] for b, h in chains]
    qs, ks, vs = [], [], []
    for b, h in chains:
        sl = slice(h * HEAD_DK, (h + 1) * HEAD_DK)
        q = p_ref[b, :, sl]
        k = p_ref[b, :, w + h * HEAD_DK:w + (h + 1) * HEAD_DK] * (HEAD_DK ** -0.5)
        qs.append(q * cos + pltpu.roll(q, HEAD_DK // 2, 1) * sin)
        ks.append(k * cos + pltpu.roll(k, HEAD_DK // 2, 1) * sin)
        vs.append(p_ref[b, :, 2 * w + h * HEAD_DK:2 * w + (h + 1) * HEAD_DK])
    scores = [_mm_nt(q, k) * dmat_ref[h] for q, k, (b, h) in zip(qs, ks, chains)]
    ys = [_mm(jnp.concatenate([sc, q * qdec_ref[:, h * HEAD_DK:(h + 1) * HEAD_DK]], axis=1),
              jnp.concatenate([v, st], axis=0))
          for sc, q, v, st, (b, h) in zip(scores, qs, vs, states, chains)]
    for i, (b, h) in enumerate(chains):
        sl = slice(h * HEAD_DK, (h + 1) * HEAD_DK)
        s_ref[b * N_HEADS + h] = (states[i] * jnp.exp(c * lg[:, sl])
                                  + _mm_tn(ks[i] * kdec_ref[:, sl], vs[i]))
        y = ys[i]
        if reverse:
            y = y + yf_ref[b, :, sl]
            g = p_ref[b, :, 3 * w + h * HEAD_DK:3 * w + (h + 1) * HEAD_DK]
            y = y * lax.rsqrt(jnp.mean(y * y, axis=-1, keepdims=True) + EPS) * _silu(g)
        o_ref[b, :, sl] = y


def _retention(p, cos, sin, log_gamma_rows, ctx_len):
    bsz, total, _ = p.shape
    w = N_HEADS * HEAD_DK
    c = RET_CHUNK
    n_all, n_ctx = total // c, ctx_len // c
    y = None
    for reverse in (False, True):
        def blk(s, reverse=reverse):
            return (0, _scan_block(s, n_ctx, n_all, reverse), 0)

        def tab(s, reverse=reverse):
            return (_scan_block(s, n_ctx, n_all, reverse), 0)

        in_specs = [pl.BlockSpec((bsz, c, 4 * w), blk),
                    pl.BlockSpec((c, HEAD_DK), tab),
                    pl.BlockSpec((c, HEAD_DK), tab),
                    _const_spec((1, w))]
        args = [p, cos, sin, log_gamma_rows[int(reverse)]]
        if reverse:
            in_specs.append(pl.BlockSpec((bsz, c, w), blk))
            args.append(y)
        y = pl.pallas_call(
            functools.partial(_retention_kernel, reverse=reverse),
            grid=(n_all,),
            in_specs=in_specs,
            out_specs=pl.BlockSpec((bsz, c, w), blk),
            out_shape=jax.ShapeDtypeStruct((bsz, total, w), F32),
            scratch_shapes=[pltpu.VMEM((bsz * N_HEADS, HEAD_DK, HEAD_DK), F32),
                            pltpu.VMEM((N_HEADS, c, c), F32),
                            pltpu.VMEM((c, w), F32), pltpu.VMEM((c, w), F32)],
            compiler_params=_params(1),
            name="retention_bwd" if reverse else "retention_fwd",
        )(*args)
    return y


def _lru_kernel(x_ref, xp_ref, xn_ref, cw_ref, cb_ref, gw_ref, gb_ref, lam_ref, *rest,
                reverse, ctx_len, total, n_ctx, n_all):
    if reverse:
        hf_ref, o_ref, buf_ref, carry_ref = rest
    else:
        o_ref, buf_ref, carry_ref = rest
    rows = LRU_BLOCK_ROWS
    s = pl.program_id(1)
    blk = _scan_block(s, n_ctx, n_all, reverse)

    @pl.when(s == 0)
    def _():
        carry_ref[...] = jnp.zeros_like(carry_ref)

    w = x_ref.shape[2] // 2
    _fill_halo(buf_ref, x_ref[0, :, 0:w], xp_ref[0], xn_ref[0], blk * rows, rows, ctx_len, total)
    xc = cb_ref[...] + sum(cw_ref[j:j + 1, :] * _shifted(buf_ref, j - CONV_K // 2, rows) for j in range(CONV_K))
    n_blocks = gw_ref.shape[0]
    bw = w // n_blocks
    parts = [_mm(xc[:, n * bw:(n + 1) * bw], gw_ref[n]) for n in range(n_blocks)]
    r_gate = _sigmoid(jnp.concatenate([p[:, :bw] for p in parts], axis=1) + gb_ref[0:1, :])
    i_gate = _sigmoid(jnp.concatenate([p[:, bw:] for p in parts], axis=1) + gb_ref[1:2, :])
    log_a = LRU_C * r_gate * (-_softplus(-lam_ref[...]))
    a = jnp.exp(log_a)
    b = jnp.sqrt(1.0 - jnp.exp(2.0 * log_a)) * i_gate * xc
    n_groups = rows // HALO
    a = a.reshape(n_groups, HALO, w)
    b = b.reshape(n_groups, HALO, w)
    rin = _iota2((1, HALO, 1), 1)
    d = 1
    while d < HALO:
        if reverse:
            a_s, b_s, ok = pltpu.roll(a, HALO - d, 1), pltpu.roll(b, HALO - d, 1), rin < HALO - d
        else:
            a_s, b_s, ok = pltpu.roll(a, d, 1), pltpu.roll(b, d, 1), rin >= d
        b = b + a * jnp.where(ok, b_s, 0.0)
        a = a * jnp.where(ok, a_s, 1.0)
        d *= 2
    carry = carry_ref[...]
    groups = [None] * n_groups
    for g in (range(n_groups - 1, -1, -1) if reverse else range(n_groups)):
        hg = b[g] + a[g] * carry
        carry = hg[0:1, :] if reverse else hg[HALO - 1:HALO, :]
        groups[g] = hg
    h = jnp.concatenate(groups, axis=0)
    carry_ref[...] = carry
    if reverse:
        h = (h + hf_ref[0]) * _gelu_tanh(x_ref[0, :, w:2 * w])
    o_ref[0] = h


def _rglru(p, conv_w, conv_b, gate_w, gate_b, lam, ctx_len):
    bsz, total, w2 = p.shape
    w = w2 // 2
    rows = LRU_BLOCK_ROWS
    n_all, n_ctx = total // rows, ctx_len // rows
    per = rows // HALO
    n8 = total // HALO
    n_blocks, bw = gate_w.shape[2], gate_w.shape[3]
    gw = jnp.transpose(gate_w, (0, 2, 3, 1, 4)).reshape(N_DIR, n_blocks, bw, 2 * bw).astype(BF16)
    h = None
    for reverse in (False, True):
        def blk(b, s, reverse=reverse):
            return (b, _scan_block(s, n_ctx, n_all, reverse), 0)

        def prev(b, s, reverse=reverse):
            return (b, jnp.maximum(_scan_block(s, n_ctx, n_all, reverse) * per - 1, 0), 0)

        def nxt(b, s, reverse=reverse):
            return (b, jnp.minimum((_scan_block(s, n_ctx, n_all, reverse) + 1) * per, n8 - 1), 0)

        d = int(reverse)
        in_specs = [pl.BlockSpec((1, rows, w2), blk),
                    pl.BlockSpec((1, HALO, w), prev),
                    pl.BlockSpec((1, HALO, w), nxt),
                    _const_spec((CONV_K, w)), _const_spec((1, w)),
                    _const_spec((n_blocks, bw, 2 * bw)), _const_spec((2, w)), _const_spec((1, w))]
        args = [p, p, p, conv_w, conv_b.reshape(1, w), gw[d], gate_b[d], lam[d].reshape(1, w)]
        if reverse:
            in_specs.append(pl.BlockSpec((1, rows, w), blk))
            args.append(h)
        h = pl.pallas_call(
            functools.partial(_lru_kernel, reverse=reverse, ctx_len=ctx_len, total=total,
                              n_ctx=n_ctx, n_all=n_all),
            grid=(bsz, n_all),
            in_specs=in_specs,
            out_specs=pl.BlockSpec((1, rows, w), blk),
            out_shape=jax.ShapeDtypeStruct((bsz, total, w), F32),
            scratch_shapes=[pltpu.VMEM((rows + 2 * HALO, w), F32), pltpu.VMEM((1, w), F32)],
            compiler_params=_params(),
            name="rglru_bwd" if reverse else "rglru_fwd",
        )(*args)
    return h


def _gdn_kernel(x_ref, xp_ref, xn_ref, abt_ref, cw_ref, alog_r_ref, dtb_r_ref, alog_c_ref, dtb_c_ref,
                ng_ref, *rest, reverse, ctx_len, total, n_ctx, n_all):
    if reverse:
        yf_ref, o_ref, buf_ref, s_ref = rest
    else:
        o_ref, buf_ref, s_ref = rest
    c = GDN_CHUNK
    s = pl.program_id(0)
    blk = _scan_block(s, n_ctx, n_all, reverse)
    bsz = x_ref.shape[0]

    @pl.when(s == 0)
    def _():
        s_ref[...] = jnp.zeros_like(s_ref)

    w = N_HEADS * HEAD_DK
    ii = _iota2((c, c), 0)
    jj = _iota2((c, c), 1)
    incl = (ii <= jj) if reverse else (ii >= jj)
    strict = (ii < jj) if reverse else (ii > jj)
    incl_t = (ii >= jj) if reverse else (ii <= jj)
    last = 0 if reverse else c - 1

    chains = [(b, h) for b in range(bsz) for h in range(N_HEADS)]
    states = [s_ref[b * N_HEADS + h] for b, h in chains]
    qs, ks, vbs, kbes, decays, e_gcs, g_lasts, gccs = [], [], [], [], [], [], [], []
    for b in range(bsz):
        _fill_halo(buf_ref.at[b], x_ref[b, :, 0:3 * w], xp_ref[b], xn_ref[b], blk * c, c, ctx_len, total)
        qkv = _silu(sum(cw_ref[j:j + 1, :] * _shifted(buf_ref.at[b], j - CONV_K // 2, c) for j in range(CONV_K)))
        ab = x_ref[b, :, 4 * w:4 * w + LANES]
        abt = abt_ref[b]
        g_col = -jnp.exp(alog_r_ref[...]) * _softplus(ab + dtb_r_ref[...])
        g_row = -jnp.exp(alog_c_ref[...]) * _softplus(abt + dtb_c_ref[...])
        beta_col = _sigmoid(ab)
        gc_col = _tri_mm(jnp.where(incl, 1.0, 0.0), g_col)
        gc_row = _mm_tri(g_row, jnp.where(incl_t, 1.0, 0.0))
        for h in range(N_HEADS):
            col = int(reverse) * N_HEADS + h
            gcc = gc_col[:, col:col + 1]
            gcr = gc_row[col:col + 1, :]
            beta = beta_col[:, 2 * N_HEADS + col:2 * N_HEADS + col + 1]
            decays.append(jnp.where(incl, jnp.exp(jnp.where(incl, gcc - gcr, 0.0)), 0.0))
            q = qkv[:, h * HEAD_DK:(h + 1) * HEAD_DK]
            k = qkv[:, w + h * HEAD_DK:w + (h + 1) * HEAD_DK]
            v = qkv[:, 2 * w + h * HEAD_DK:2 * w + (h + 1) * HEAD_DK]
            qs.append(q * lax.rsqrt(jnp.sum(q * q, axis=-1, keepdims=True) + EPS) * (HEAD_DK ** -0.5))
            k = k * lax.rsqrt(jnp.sum(k * k, axis=-1, keepdims=True) + EPS)
            ks.append(k)
            kbes.append(k * beta)
            vbs.append(v * beta)
            gccs.append(gcc)
            e_gcs.append(jnp.exp(gcc))
            g_lasts.append(gcc[last:last + 1, :])

    kq = [_mm_nt(jnp.concatenate([kb, q], axis=0), k) for kb, q, k in zip(kbes, qs, ks)]
    t_invs = _tri_inverse([jnp.where(strict, m[0:c] * dec, 0.0) for m, dec in zip(kq, decays)], c)
    qks = [m[c:2 * c] * dec for m, dec in zip(kq, decays)]
    uw = [_mm(t, jnp.concatenate([vb, kb * e], axis=1)) for t, vb, kb, e in zip(t_invs, vbs, kbes, e_gcs)]
    ws = [_mm(jnp.concatenate([m[:, HEAD_DK:], q * e], axis=0), st)
          for m, q, e, st in zip(uw, qs, e_gcs, states)]
    v_news = [m[:, :HEAD_DK] - x[0:c] for m, x in zip(uw, ws)]
    ys = [x[c:2 * c] + _mm(qk, vn) for x, qk, vn in zip(ws, qks, v_news)]
    new_states = [st * jnp.exp(gl) + _mm_tn(k * jnp.exp(gl - gcc), vn)
                  for st, gl, k, gcc, vn in zip(states, g_lasts, ks, gccs, v_news)]
    for i, (b, h) in enumerate(chains):
        s_ref[b * N_HEADS + h] = new_states[i]
        y = ys[i]
        if reverse:
            y = y + yf_ref[b, :, h * HEAD_DK:(h + 1) * HEAD_DK]
            z = x_ref[b, :, 3 * w + h * HEAD_DK:3 * w + (h + 1) * HEAD_DK]
            y = y * lax.rsqrt(jnp.mean(y * y, axis=-1, keepdims=True) + EPS) * ng_ref[...] * _silu(z)
        o_ref[b, :, h * HEAD_DK:(h + 1) * HEAD_DK] = y


def _gated_deltanet(p, abt, conv_w, a_log, dt_bias, norm_g, ctx_len):
    bsz, total, pw = p.shape
    w = N_HEADS * HEAD_DK
    c = GDN_CHUNK
    n_all, n_ctx = total // c, ctx_len // c
    per = c // HALO
    n8 = total // HALO
    nt = abt.shape[1]
    flat_alog = a_log.reshape(-1)
    flat_dtb = dt_bias.reshape(-1)
    alog_r = jnp.zeros((1, LANES), F32).at[0, :flat_alog.shape[0]].set(flat_alog)
    dtb_r = jnp.zeros((1, LANES), F32).at[0, :flat_dtb.shape[0]].set(flat_dtb)
    alog_c = jnp.zeros((nt, 1), F32).at[:flat_alog.shape[0], 0].set(flat_alog)
    dtb_c = jnp.zeros((nt, 1), F32).at[:flat_dtb.shape[0], 0].set(flat_dtb)
    y = None
    for reverse in (False, True):
        def blk(s, reverse=reverse):
            return (0, _scan_block(s, n_ctx, n_all, reverse), 0)

        def blk_t(s, reverse=reverse):
            return (0, 0, _scan_block(s, n_ctx, n_all, reverse))

        def prev(s, reverse=reverse):
            return (0, jnp.maximum(_scan_block(s, n_ctx, n_all, reverse) * per - 1, 0), 0)

        def nxt(s, reverse=reverse):
            return (0, jnp.minimum((_scan_block(s, n_ctx, n_all, reverse) + 1) * per, n8 - 1), 0)

        in_specs = [pl.BlockSpec((bsz, c, pw), blk),
                    pl.BlockSpec((bsz, HALO, 3 * w), prev),
                    pl.BlockSpec((bsz, HALO, 3 * w), nxt),
                    pl.BlockSpec((bsz, nt, c), blk_t),
                    _const_spec((CONV_K, 3 * w)),
                    _const_spec((1, LANES)), _const_spec((1, LANES)),
                    _const_spec((nt, 1)), _const_spec((nt, 1)),
                    _const_spec((1, HEAD_DK))]
        args = [p, p, p, abt, conv_w, alog_r, dtb_r, alog_c, dtb_c, norm_g.reshape(1, HEAD_DK)]
        if reverse:
            in_specs.append(pl.BlockSpec((bsz, c, w), blk))
            args.append(y)
        y = pl.pallas_call(
            functools.partial(_gdn_kernel, reverse=reverse, ctx_len=ctx_len, total=total,
                              n_ctx=n_ctx, n_all=n_all),
            grid=(n_all,),
            in_specs=in_specs,
            out_specs=pl.BlockSpec((bsz, c, w), blk),
            out_shape=jax.ShapeDtypeStruct((bsz, total, w), F32),
            scratch_shapes=[pltpu.VMEM((bsz, c + 2 * HALO, 3 * w), F32),
                            pltpu.VMEM((bsz * N_HEADS, HEAD_DK, HEAD_DK), F32)],
            compiler_params=_params(1),
            name="gdn_bwd" if reverse else "gdn_fwd",
        )(*args)
    return y


def _rwkv_kernel(x_ref, xp_ref, xn_ref, mu_ref, w0_ref, w2_ref, a0_ref, a2_ref, kk_ref, ka_ref, hs_ref, *rest,
                 reverse, ctx_len, total, n_ctx, n_all):
    if reverse:
        g2_ref, rk_ref, lng_ref, lnb_ref, yf_ref, o_ref, buf_ref, s_ref = rest
    else:
        o_ref, buf_ref, s_ref = rest
    c = RWKV_CHUNK
    step = pl.program_id(0)
    blk = _scan_block(step, n_ctx, n_all, reverse)
    bsz = x_ref.shape[0]
    n_pairs = s_ref.shape[0] // bsz
    bw = n_pairs * LANES

    @pl.when(step == 0)
    def _():
        s_ref[...] = jnp.zeros_like(s_ref)

    rows = []
    for b in range(bsz):
        xb = x_ref[b]
        _fill_halo(buf_ref.at[b], xb, xp_ref[b], xn_ref[b], blk * c, c, ctx_len, total)
        rows.append(xb + (0.5 * (_shifted(buf_ref.at[b], -1, c) + _shifted(buf_ref.at[b], 1, c)) - xb) * mu_ref[...])
    x = jnp.concatenate(rows, axis=0)
    n = bsz * c
    r = x[:, 0:bw]
    k = x[:, bw:2 * bw]
    v = x[:, 2 * bw:3 * bw]
    off = 3 * bw
    gd = x[:, off:off + RWKV_GATE_RANK]
    off += RWKV_GATE_RANK
    wd = [x[:, off + d * RWKV_RANK:off + (d + 1) * RWKV_RANK] for d in range(N_DIR)]
    off += N_DIR * RWKV_RANK
    ad = [x[:, off + d * RWKV_RANK:off + (d + 1) * RWKV_RANK] for d in range(N_DIR)]

    def per_head_sum(t):
        t1, t2, t3 = _split3(t)
        hs = hs_ref[...]
        return (jnp.dot(t1, hs, preferred_element_type=F32) + jnp.dot(t2, hs, preferred_element_type=F32)
                + jnp.dot(t3, hs, preferred_element_type=F32))

    def a_of(d):
        return _sigmoid(a0_ref[d:d + 1, :] + _mm(ad[d], a2_ref[d]))

    d = int(reverse)
    w_raw = -_softplus(-(w0_ref[d:d + 1, :] + _mm(jnp.tanh(wd[d]), w2_ref[d]))) - 0.5
    logw = -jnp.exp(w_raw)
    a_dir = a_of(d)
    kkv = k * kk_ref[...]
    kkv = kkv * lax.rsqrt(per_head_sum(kkv * kkv) + EPS)
    kd = k * (1.0 + (a_dir - 1.0) * ka_ref[...])

    cshift = c.bit_length() - 1

    def chunk_masks(size):
        ii = _iota2((size, size), 0)
        jj = _iota2((size, size), 1)
        same = (ii >> cshift) == (jj >> cshift)
        return ((ii <= jj) if reverse else (ii >= jj)) & same, ((ii < jj) if reverse else (ii > jj)) & same

    cum = _tri_mm(jnp.where(chunk_masks(n)[0], 1.0, 0.0), logw)
    last = 0 if reverse else c - 1
    cum_tot = jnp.concatenate([jnp.broadcast_to(cum[b * c + last:b * c + last + 1, :], (c, bw))
                               for b in range(bsz)], axis=0)
    w_inv = jnp.exp(-cum)
    w_rest = jnp.exp(cum_tot - cum)
    alpha_t = -kkv * jnp.exp(cum - logw)
    beta = kkv * a_dir
    beta_t = beta * w_inv
    k_t = kd * w_inv
    r_t = r * jnp.exp(cum)
    beta_h = beta * w_rest
    k_h = kd * w_rest

    n2 = 2 * c
    incl, strict = chunk_masks(n2)
    lane = _iota2((1, LANES), 1)
    m0 = jnp.where(lane < RWKV_N, 1.0, 0.0)
    m1 = 1.0 - m0
    incl2 = jnp.concatenate([incl, incl], axis=1)

    def stack(t, b, p):
        t = t[b * c:(b + 1) * c, p * LANES:(p + 1) * LANES]
        return jnp.concatenate([t * m0, t * m1], axis=0)

    chains = [(b, p) for b in range(bsz) for p in range(n_pairs)]
    states = [s_ref[b * n_pairs + p] for b, p in chains]
    ar = [jnp.concatenate([stack(alpha_t, b, p), stack(r_t, b, p)], axis=0) for b, p in chains]
    bk = [jnp.concatenate([stack(beta_t, b, p), stack(k_t, b, p)], axis=0) for b, p in chains]
    vs = [stack(v, b, p) for b, p in chains]
    bkh = [jnp.concatenate([stack(beta_h, b, p), stack(k_h, b, p)], axis=0) for b, p in chains]
    g = [_mm_nt(x1, x2) for x1, x2 in zip(ar, bk)]
    t_invs = _tri_inverse([jnp.where(strict, -m[0:n2, 0:n2], 0.0) for m in g], c)
    mv = [_mm(jnp.where(strict, m[0:n2, n2:2 * n2], 0.0), vv) for m, vv in zip(g, vs)]
    xs = [_mm_nt(x1, st) for x1, st in zip(ar, states)]
    us = [_mm(t, x1[0:n2] + m) for t, x1, m in zip(t_invs, xs, mv)]
    uv = [jnp.concatenate([u, vv], axis=0) for u, vv in zip(us, vs)]
    ys = [x1[n2:2 * n2] + _mm(jnp.where(incl2, m[n2:2 * n2, :], 0.0), w) for x1, m, w in zip(xs, g, uv)]
    for i, (b, p) in enumerate(chains):
        w_tot = jnp.exp(cum_tot[b * c:b * c + 1, p * LANES:(p + 1) * LANES])
        s_ref[b * n_pairs + p] = states[i] * w_tot + _mm_tn(uv[i], bkh[i])
    y = jnp.concatenate([jnp.concatenate([ys[b * n_pairs + p][0:c] + ys[b * n_pairs + p][c:n2]
                                          for p in range(n_pairs)], axis=1) for b in range(bsz)], axis=0)
    if reverse:
        y = y + jnp.concatenate([yf_ref[b] for b in range(bsz)], axis=0)
        mean = per_head_sum(y) * (1.0 / RWKV_N)
        yc = y - mean
        var = per_head_sum(yc * yc) * (1.0 / RWKV_N)
        yn = yc * lax.rsqrt(var + RWKV_LN_EPS) * lng_ref[...] + lnb_ref[...]
        k_sum = kd + k * (1.0 + (a_of(0) - 1.0) * ka_ref[...])
        bonus = per_head_sum(r * k_sum * rk_ref[...]) * v
        y = (yn + bonus) * _mm(_sigmoid(gd), g2_ref[...])
    for b in range(bsz):
        o_ref[b] = y[b * c:(b + 1) * c]


def _rwkv7(p, mu, w0, w2, a0, a2, g2, k_k, k_a, r_k, ln_g, ln_b, ctx_len):
    bsz, total, pw = p.shape
    bw = k_k.shape[0]
    c = RWKV_CHUNK
    n_all, n_ctx = total // c, ctx_len // c
    per = c // HALO
    n8 = total // HALO
    head_id = jnp.arange(bw) // RWKV_N
    head_sum = (head_id[:, None] == head_id[None, :]).astype(BF16)
    y = None
    for reverse in (False, True):
        def blk(s, reverse=reverse):
            return (0, _scan_block(s, n_ctx, n_all, reverse), 0)

        def prev(s, reverse=reverse):
            return (0, jnp.maximum(_scan_block(s, n_ctx, n_all, reverse) * per - 1, 0), 0)

        def nxt(s, reverse=reverse):
            return (0, jnp.minimum((_scan_block(s, n_ctx, n_all, reverse) + 1) * per, n8 - 1), 0)

        in_specs = [pl.BlockSpec((bsz, c, pw), blk),
                    pl.BlockSpec((bsz, HALO, pw), prev),
                    pl.BlockSpec((bsz, HALO, pw), nxt),
                    _const_spec((1, pw)),
                    _const_spec((N_DIR, bw)), _const_spec((N_DIR, RWKV_RANK, bw)),
                    _const_spec((N_DIR, bw)), _const_spec((N_DIR, RWKV_RANK, bw)),
                    _const_spec((1, bw)), _const_spec((1, bw)), _const_spec((bw, bw))]
        args = [p, p, p, mu.reshape(1, pw), w0, w2.astype(BF16), a0, a2.astype(BF16),
                k_k.reshape(1, bw), k_a.reshape(1, bw), head_sum]
        if reverse:
            in_specs += [_const_spec((RWKV_GATE_RANK, bw)), _const_spec((1, bw)), _const_spec((1, bw)),
                         _const_spec((1, bw)), pl.BlockSpec((bsz, c, bw), blk)]
            args += [g2.astype(BF16), r_k.reshape(1, bw), ln_g.reshape(1, bw), ln_b.reshape(1, bw), y]
        y = pl.pallas_call(
            functools.partial(_rwkv_kernel, reverse=reverse, ctx_len=ctx_len, total=total,
                              n_ctx=n_ctx, n_all=n_all),
            grid=(n_all,),
            in_specs=in_specs,
            out_specs=pl.BlockSpec((bsz, c, bw), blk),
            out_shape=jax.ShapeDtypeStruct((bsz, total, bw), F32),
            scratch_shapes=[pltpu.VMEM((bsz, c + 2 * HALO, pw), F32),
                            pltpu.VMEM((bsz * (bw // LANES), LANES, LANES), F32)],
            compiler_params=_params(1),
            name="rwkv_bwd" if reverse else "rwkv_fwd",
        )(*args)
    return y


def _merge_kernel(h_ref, xm_ref, mod_ref, wg_ref, gb_ref, y0_ref, y1_ref, y2_ref, y3_ref, bw_ref, ow_ref,
                  o_ref, *, rows, ctx_len):
    tile = pl.program_id(1)
    h = h_ref[0]
    d = h.shape[1]
    xm = xm_ref[0]
    merged = jnp.zeros_like(h)
    for k, y_ref in enumerate((y0_ref, y1_ref, y2_ref, y3_ref)):
        gate = _sigmoid(jnp.dot(xm, wg_ref[:, k * d:(k + 1) * d], preferred_element_type=F32)
                        + gb_ref[:, k * d:(k + 1) * d])
        merged = merged + gate * _mm(y_ref[0], bw_ref[k])
    o_ref[0] = h + _mod_row(mod_ref, tile, rows, ctx_len, 2) * _mm(merged, ow_ref[...])


def _merge(h, xm, modtab, w_gates, gate_b, ys, branch_w, out_w, ctx_len):
    bsz, total, d = h.shape
    rows = _token_tile(total, TOKEN_TILES)
    bwid = ys[0].shape[2]
    row_spec = lambda width: pl.BlockSpec((1, rows, width), lambda b, i: (b, i, 0))
    return pl.pallas_call(
        functools.partial(_merge_kernel, rows=rows, ctx_len=ctx_len),
        grid=(bsz, total // rows),
        in_specs=[row_spec(d), row_spec(d),
                  pl.BlockSpec((1, 2, 6, d), lambda b, i: (b, 0, 0, 0)),
                  _weight_spec((d, N_BRANCH * d)), _const_spec((1, N_BRANCH * d)),
                  row_spec(bwid), row_spec(bwid), row_spec(bwid), row_spec(bwid),
                  _weight_spec((N_BRANCH, bwid, d)), _weight_spec((d, d))],
        out_specs=row_spec(d),
        out_shape=jax.ShapeDtypeStruct((bsz, total, d), F32),
        compiler_params=_params(),
        name="merge",
    )(h, xm, modtab, w_gates, gate_b.reshape(1, -1), *ys, branch_w, out_w)


def _ffn_kernel(h_ref, g_ref, mod_ref, w1_ref, w3_ref, w2_ref, fg_ref, o_ref, *, rows, ctx_len, final):
    tile = pl.program_id(1)
    h = h_ref[0]
    u = _norm_modulate(h, g_ref[...], mod_ref, tile, rows, ctx_len, 3, 4).astype(BF16)
    dff = w1_ref.shape[1]
    part = dff // FFN_SPLIT
    out = jnp.zeros_like(h)
    for j in range(FFN_SPLIT):
        cols = slice(j * part, (j + 1) * part)
        t = (_silu(jnp.dot(u, w1_ref[:, cols], preferred_element_type=F32))
             * jnp.dot(u, w3_ref[:, cols], preferred_element_type=F32))
        out = out + _mm(t, w2_ref[cols, :])
    h = h + _mod_row(mod_ref, tile, rows, ctx_len, 5) * out
    if final:
        h = h * lax.rsqrt(jnp.mean(h * h, axis=-1, keepdims=True) + EPS) * fg_ref[...]
    o_ref[0] = h


def _ffn(h, g, modtab, w1, w3, w2, final_g, ctx_len, final):
    bsz, total, d = h.shape
    rows = _token_tile(total, TOKEN_TILES)
    dff = w1.shape[1]
    row_spec = pl.BlockSpec((1, rows, d), lambda b, i: (b, i, 0))
    return pl.pallas_call(
        functools.partial(_ffn_kernel, rows=rows, ctx_len=ctx_len, final=final),
        grid=(bsz, total // rows),
        in_specs=[row_spec, _const_spec((1, d)),
                  pl.BlockSpec((1, 2, 6, d), lambda b, i: (b, 0, 0, 0)),
                  _weight_spec((d, dff)), _weight_spec((d, dff)), _weight_spec((dff, d)), _const_spec((1, d))],
        out_specs=row_spec,
        out_shape=jax.ShapeDtypeStruct((bsz, total, d), F32),
        compiler_params=_params(),
        name="ffn",
    )(h, g.reshape(1, d), modtab, w1, w3, w2, final_g.reshape(1, d))


def _rope_tables(seq, ctx_len):
    t = jnp.arange(seq)
    row = (t // GRID_W).astype(F32)
    col = (t % GRID_W).astype(F32)
    quarter = HEAD_DK // 4
    inv = ROPE_BASE ** (-jnp.arange(quarter, dtype=F32) / quarter)
    ang = jnp.concatenate([row[:, None] * inv, col[:, None] * inv], axis=-1)
    cos, sin = jnp.cos(ang), jnp.sin(ang)
    cos = jnp.concatenate([cos, cos], axis=-1)
    sin = jnp.concatenate([-sin, sin], axis=-1)
    cos = jnp.concatenate([jnp.ones((ctx_len, HEAD_DK), F32), cos], axis=0)
    sin = jnp.concatenate([jnp.zeros((ctx_len, HEAD_DK), F32), sin], axis=0)
    return cos, sin


def kernel(x, c, ctx, c_ctx, mod_w, mod_b, norm1_g, norm2_g, in_w, gate_b, ret_decay_exp, lru_conv_w,
           lru_conv_b, lru_gate_w, lru_gate_b, lru_lambda, gdn_conv_w, gdn_a_log, gdn_dt_bias, gdn_norm_g,
           rwkv_mu, rwkv_w0, rwkv_w2, rwkv_a0, rwkv_a2, rwkv_g2, rwkv_k_k, rwkv_k_a, rwkv_r_k, rwkv_ln_g,
           rwkv_ln_b, branch_w, out_w, ffn_w1, ffn_w3, ffn_w2, final_norm_g):
    bsz, seq, d = x.shape
    ctx_len = ctx.shape[1]
    depth = mod_w.shape[0]
    bw = d // 2
    assert ctx_len % LRU_BLOCK_ROWS == 0 and seq % LRU_BLOCK_ROWS == 0 and bsz + 1 <= HALO

    h = jnp.concatenate([ctx, x], axis=1)
    cond = jnp.zeros((HALO, d), F32).at[:bsz].set(c).at[bsz].set(c_ctx)
    mods = _modulation(cond, mod_w, mod_b).reshape(depth, HALO, 6, d)
    cos, sin = _rope_tables(seq, ctx_len)

    widths = (4 * bw, 2 * bw, 3 * bw, bw, N_DIR * N_HEADS, N_DIR * N_HEADS,
              3 * bw + RWKV_GATE_RANK + 2 * N_DIR * RWKV_RANK, N_BRANCH * d)
    offs = [0]
    for wdt in widths:
        offs.append(offs[-1] + wdt)

    for l in range(depth):
        lat = mods[l, :bsz]
        modtab = jnp.stack([jnp.broadcast_to(mods[l, bsz], lat.shape), lat], axis=1)
        wl = in_w[l]
        w_ret = wl[:, offs[0]:offs[1]].astype(BF16)
        w_lru = wl[:, offs[1]:offs[2]].astype(BF16)
        w_ab = wl[:, offs[4]:offs[6]]
        w_gdn = jnp.concatenate([wl[:, offs[2]:offs[4]], w_ab,
                                 jnp.zeros((d, LANES - w_ab.shape[1]), F32)], axis=1).astype(BF16)
        w_abt = w_ab.T.astype(BF16)
        w_rwkv = wl[:, offs[6]:offs[7]].astype(BF16)
        w_gates = wl[:, offs[7]:offs[8]].astype(BF16)

        xm = _mixer_input(h, norm1_g[l], modtab, ctx_len)
        p_ret = _in_projection(xm, w_ret)
        p_lru = _in_projection(xm, w_lru)
        p_gdn, abt = _in_projection(xm, w_gdn, w_t=w_abt)
        p_rwkv = _in_projection(xm, w_rwkv)

        log_gamma = jnp.log1p(-jnp.exp2(-ret_decay_exp[l].astype(F32)))
        lg_rows = jnp.repeat(log_gamma, HEAD_DK, axis=1).reshape(N_DIR, 1, N_HEADS * HEAD_DK)
        y_ret = _retention(p_ret, cos, sin, lg_rows, ctx_len)
        y_lru = _rglru(p_lru, lru_conv_w[l], lru_conv_b[l], lru_gate_w[l], lru_gate_b[l], lru_lambda[l], ctx_len)
        y_gdn = _gated_deltanet(p_gdn, abt, gdn_conv_w[l], gdn_a_log[l], gdn_dt_bias[l], gdn_norm_g[l], ctx_len)
        y_rwkv = _rwkv7(p_rwkv, rwkv_mu[l], rwkv_w0[l], rwkv_w2[l], rwkv_a0[l], rwkv_a2[l], rwkv_g2[l],
                        rwkv_k_k[l], rwkv_k_a[l], rwkv_r_k[l].reshape(-1), rwkv_ln_g[l], rwkv_ln_b[l], ctx_len)

        h = _merge(h, xm, modtab, w_gates, gate_b[l], (y_ret, y_lru, y_gdn, y_rwkv),
                   branch_w[l].astype(BF16), out_w[l].astype(BF16), ctx_len)
        h = _ffn(h, norm2_g[l], modtab, ffn_w1[l].astype(BF16), ffn_w3[l].astype(BF16),
                 ffn_w2[l].astype(BF16), final_norm_g, ctx_len, final=(l == depth - 1))
    return h[:, ctx_len:]
```

```python
import functools

import jax
import jax.numpy as jnp
from jax import lax
from jax.experimental import pallas as pl
from jax.experimental.pallas import tpu as pltpu

F32 = jnp.float32
BF16 = jnp.bfloat16

GRID_W = 64
ROPE_BASE = 10000.0
EPS = 1e-6
RWKV_LN_EPS = 64e-5
LRU_C = 8.0
N_DIR = 2
N_BRANCH = 4
HEAD_DK = 128
N_HEADS = 4
RWKV_N = 64
RWKV_RANK = 64
RWKV_GATE_RANK = 128
CONV_K = 4
LANES = 128
HALO = 8
VMEM_LIMIT_BYTES = 56 * 1024 * 1024

TOKEN_TILES = (768, 512, 256)
PROJ_TILES = (1408,) + TOKEN_TILES
RET_CHUNK = 128
LRU_BLOCK_ROWS = 256
GDN_CHUNK = 128
RWKV_CHUNK = 64
FFN_SPLIT = 2


def _mm(a, b):
    return jnp.dot(a.astype(BF16), b.astype(BF16), preferred_element_type=F32)


def _mm_nt(a, b):
    return lax.dot_general(a.astype(BF16), b.astype(BF16), (((1,), (1,)), ((), ())),
                           preferred_element_type=F32)


def _mm_tn(a, b):
    return lax.dot_general(a.astype(BF16), b.astype(BF16), (((0,), (0,)), ((), ())),
                           preferred_element_type=F32)


def _split3(x):
    x1 = x.astype(BF16)
    r1 = x - x1.astype(F32)
    x2 = r1.astype(BF16)
    x3 = (r1 - x2.astype(F32)).astype(BF16)
    return x1, x2, x3


def _tri_mm(tri, x):
    x1, x2, x3 = _split3(x)
    t = tri.astype(BF16)
    return (jnp.dot(t, x1, preferred_element_type=F32) + jnp.dot(t, x2, preferred_element_type=F32)
            + jnp.dot(t, x3, preferred_element_type=F32))


def _mm_tri(x, tri):
    x1, x2, x3 = _split3(x)
    t = tri.astype(BF16)
    return (jnp.dot(x1, t, preferred_element_type=F32) + jnp.dot(x2, t, preferred_element_type=F32)
            + jnp.dot(x3, t, preferred_element_type=F32))


def _sigmoid(x):
    return 1.0 / (1.0 + jnp.exp(-x))


def _silu(x):
    return x * _sigmoid(x)


def _softplus(x):
    return jnp.maximum(x, 0.0) + jnp.log1p(jnp.exp(-jnp.abs(x)))


def _gelu_tanh(x):
    return 0.5 * x * (1.0 + jnp.tanh(0.7978845608028654 * (x + 0.044715 * x * x * x)))


def _iota2(shape, axis):
    return lax.broadcasted_iota(jnp.int32, shape, axis)


def _scan_block(s, n_ctx, n_all, reverse):
    if not reverse:
        return s
    return jnp.where(s < n_ctx, n_ctx - 1 - s, n_all - 1 - (s - n_ctx))


def _tri_inverse(mats, n_max, between=None):
    n = mats[0].shape[0]
    ii = _iota2((n, n), 0)
    jj = _iota2((n, n), 1)
    eye = jnp.where(ii == jj, 1.0, 0.0)
    pair = (ii >> 1) == (jj >> 1)
    ts = [eye - jnp.where(pair, a, 0.0) for a in mats]
    b = 2
    while b < n_max:
        sh = b.bit_length() - 1
        halves = ((ii >> (sh + 1)) == (jj >> (sh + 1))) & ((ii >> sh) != (jj >> sh))
        xs = [_mm(t, jnp.where(halves, a, 0.0)) for t, a in zip(ts, mats)]
        if between is not None:
            between()
        ts = [t - _mm(x, t) for t, x in zip(ts, xs)]
        if between is not None:
            between()
        b *= 2
    return ts


def _fill_halo(buf_ref, x, prev, nxt, row0, rows, ctx_len, total):
    prev_ok = jnp.logical_and(row0 != 0, row0 != ctx_len)
    next_ok = jnp.logical_and(row0 + rows != ctx_len, row0 + rows != total)
    buf_ref[0:HALO, :] = jnp.where(prev_ok, prev, 0.0)
    buf_ref[HALO:HALO + rows, :] = x
    buf_ref[HALO + rows:HALO + rows + HALO, :] = jnp.where(next_ok, nxt, 0.0)


def _shifted(buf_ref, off, rows):
    return buf_ref[HALO + off:HALO + off + rows, :]


def _norm_modulate(x, g, mod_ref, tile, rows, ctx_len, shift_idx, scale_idx):
    xn = x * lax.rsqrt(jnp.mean(x * x, axis=-1, keepdims=True) + EPS) * g
    is_ctx = (tile * rows + _iota2((rows, 1), 0)) < ctx_len
    shift = jnp.where(is_ctx, mod_ref[0, 0, shift_idx:shift_idx + 1, :], mod_ref[0, 1, shift_idx:shift_idx + 1, :])
    scale = jnp.where(is_ctx, mod_ref[0, 0, scale_idx:scale_idx + 1, :], mod_ref[0, 1, scale_idx:scale_idx + 1, :])
    return xn * (1.0 + scale) + shift


def _mod_row(mod_ref, tile, rows, ctx_len, idx):
    is_ctx = (tile * rows + _iota2((rows, 1), 0)) < ctx_len
    return jnp.where(is_ctx, mod_ref[0, 0, idx:idx + 1, :], mod_ref[0, 1, idx:idx + 1, :])


def _params(n_axes=2):
    return pltpu.CompilerParams(dimension_semantics=("arbitrary",) * n_axes,
                                vmem_limit_bytes=VMEM_LIMIT_BYTES)


def _const_spec(shape):
    nd = len(shape)
    return pl.BlockSpec(shape, lambda *_: (0,) * nd)


def _weight_spec(shape):
    nd = len(shape)
    return pl.BlockSpec(shape, lambda *_: (0,) * nd, pipeline_mode=pl.Buffered(1))


def _token_tile(total, candidates):
    for rows in candidates:
        if total % rows == 0:
            return rows
    raise ValueError(f"token count {total} is not a multiple of {candidates[-1]}")


def _mod_kernel(c_ref, w_ref, b_ref, o_ref):
    o_ref[0] = _mm(_silu(c_ref[...]), w_ref[0]) + b_ref[0]


def _modulation(cond, mod_w, mod_b):
    depth, d, n = mod_w.shape
    tn = n // 4
    return pl.pallas_call(
        _mod_kernel,
        grid=(depth, n // tn),
        in_specs=[pl.BlockSpec(cond.shape, lambda l, j: (0, 0)),
                  pl.BlockSpec((1, d, tn), lambda l, j: (l, 0, j)),
                  pl.BlockSpec((1, 1, tn), lambda l, j: (l, 0, j))],
        out_specs=pl.BlockSpec((1, cond.shape[0], tn), lambda l, j: (l, 0, j)),
        out_shape=jax.ShapeDtypeStruct((depth, cond.shape[0], n), F32),
        compiler_params=_params(),
        name="modulation",
    )(cond, mod_w, mod_b.reshape(depth, 1, n))


def _normmod_kernel(h_ref, g_ref, mod_ref, o_ref, *, rows, ctx_len):
    o_ref[0] = _norm_modulate(h_ref[0], g_ref[...], mod_ref, pl.program_id(1), rows, ctx_len, 0, 1).astype(BF16)


def _mixer_input(h, g, modtab, ctx_len):
    bsz, total, d = h.shape
    rows = _token_tile(total, TOKEN_TILES)
    row_spec = pl.BlockSpec((1, rows, d), lambda b, i: (b, i, 0))
    return pl.pallas_call(
        functools.partial(_normmod_kernel, rows=rows, ctx_len=ctx_len),
        grid=(bsz, total // rows),
        in_specs=[row_spec, _const_spec((1, d)), pl.BlockSpec((1, 2, 6, d), lambda b, i: (b, 0, 0, 0))],
        out_specs=row_spec,
        out_shape=jax.ShapeDtypeStruct((bsz, total, d), BF16),
        compiler_params=_params(),
        name="norm_modulate",
    )(h, g.reshape(1, d), modtab)


def _inproj_kernel(x_ref, w_ref, *rest, with_t):
    if with_t:
        wt_ref, o_ref, ot_ref = rest
    else:
        (o_ref,) = rest
    o_ref[0] = jnp.dot(x_ref[0], w_ref[...], preferred_element_type=F32)
    if with_t:
        ot_ref[0] = lax.dot_general(wt_ref[...], x_ref[0], (((1,), (1,)), ((), ())),
                                    preferred_element_type=F32)


def _in_projection(xm, w, w_t=None):
    bsz, total, d = xm.shape
    n = w.shape[1]
    rows = _token_tile(total, PROJ_TILES)
    in_specs = [pl.BlockSpec((1, rows, d), lambda b, i: (b, i, 0)),
                _weight_spec((d, n))]
    out_specs = [pl.BlockSpec((1, rows, n), lambda b, i: (b, i, 0))]
    out_shape = [jax.ShapeDtypeStruct((bsz, total, n), F32)]
    args = [xm, w]
    if w_t is not None:
        nt = w_t.shape[0]
        in_specs.append(_const_spec((nt, d)))
        out_specs.append(pl.BlockSpec((1, nt, rows), lambda b, i: (b, 0, i)))
        out_shape.append(jax.ShapeDtypeStruct((bsz, nt, total), F32))
        args.append(w_t)
    out = pl.pallas_call(
        functools.partial(_inproj_kernel, with_t=w_t is not None),
        grid=(bsz, total // rows),
        in_specs=in_specs, out_specs=out_specs, out_shape=out_shape,
        compiler_params=_params(),
        name="in_projection",
    )(*args)
    return out if w_t is not None else out[0]


def _retention_kernel(p_ref, cos_ref, sin_ref, lg_ref, *rest, reverse):
    if reverse:
        yf_ref, o_ref, s_ref, dmat_ref, qdec_ref, kdec_ref = rest
    else:
        o_ref, s_ref, dmat_ref, qdec_ref, kdec_ref = rest
    c = RET_CHUNK
    bsz = p_ref.shape[0]
    lg = lg_ref[...]

    @pl.when(pl.program_id(0) == 0)
    def _():
        s_ref[...] = jnp.zeros_like(s_ref)
        pos = _iota2((c, 1), 0).astype(F32)
        qdec_ref[...] = jnp.exp(((c - pos) if reverse else (pos + 1.0)) * lg)
        kdec_ref[...] = jnp.exp((pos if reverse else (c - 1.0 - pos)) * lg)
        ii = _iota2((c, c), 0)
        jj = _iota2((c, c), 1)
        rel = (jj - ii) if reverse else (ii - jj)
        keep = rel >= 0
        relf = jnp.where(keep, rel, 0).astype(F32)
        for h in range(N_HEADS):
            dmat_ref[h] = jnp.where(keep, jnp.exp(relf * lg[:, h * HEAD_DK:(h + 1) * HEAD_DK]), 0.0)

    cos = cos_ref[...]
    sin = sin_ref[...]
    w = N_HEADS * HEAD_DK
    chains = [(b, h) for b in range(bsz) for h in range(N_HEADS)]
    states = [s_ref[b * N_HEADS + h] for b, h in chains]
    qs, ks, vs = [], [], []
    for b, h in chains:
        sl = slice(h * HEAD_DK, (h + 1) * HEAD_DK)
        q = p_ref[b, :, sl]
        k = p_ref[b, :, w + h * HEAD_DK:w + (h + 1) * HEAD_DK] * (HEAD_DK ** -0.5)
        qs.append(q * cos + pltpu.roll(q, HEAD_DK // 2, 1) * sin)
        ks.append(k * cos + pltpu.roll(k, HEAD_DK // 2, 1) * sin)
        vs.append(p_ref[b, :, 2 * w + h * HEAD_DK:2 * w + (h + 1) * HEAD_DK])
    scores = [_mm_nt(q, k) * dmat_ref[h] for q, k, (b, h) in zip(qs, ks, chains)]
    ys = [_mm(jnp.concatenate([sc, q * qdec_ref[:, h * HEAD_DK:(h + 1) * HEAD_DK]], axis=1),
              jnp.concatenate([v, st], axis=0))
          for sc, q, v, st, (b, h) in zip(scores, qs, vs, states, chains)]
    for i, (b, h) in enumerate(chains):
        sl = slice(h * HEAD_DK, (h + 1) * HEAD_DK)
        s_ref[b * N_HEADS + h] = (states[i] * jnp.exp(c * lg[:, sl])
                                  + _mm_tn(ks[i] * kdec_ref[:, sl], vs[i]))
        y = ys[i]
        if reverse:
            y = y + yf_ref[b, :, sl]
            g = p_ref[b, :, 3 * w + h * HEAD_DK:3 * w + (h + 1) * HEAD_DK]
            y = y * lax.rsqrt(jnp.mean(y * y, axis=-1, keepdims=True) + EPS) * _silu(g)
        o_ref[b, :, sl] = y


def _retention(p, cos, sin, log_gamma_rows, ctx_len):
    bsz, total, _ = p.shape
    w = N_HEADS * HEAD_DK
    c = RET_CHUNK
    n_all, n_ctx = total // c, ctx_len // c
    y = None
    for reverse in (False, True):
        def blk(s, reverse=reverse):
            return (0, _scan_block(s, n_ctx, n_all, reverse), 0)

        def tab(s, reverse=reverse):
            return (_scan_block(s, n_ctx, n_all, reverse), 0)

        in_specs = [pl.BlockSpec((bsz, c, 4 * w), blk),
                    pl.BlockSpec((c, HEAD_DK), tab),
                    pl.BlockSpec((c, HEAD_DK), tab),
                    _const_spec((1, w))]
        args = [p, cos, sin, log_gamma_rows[int(reverse)]]
        if reverse:
            in_specs.append(pl.BlockSpec((bsz, c, w), blk))
            args.append(y)
        y = pl.pallas_call(
            functools.partial(_retention_kernel, reverse=reverse),
            grid=(n_all,),
            in_specs=in_specs,
            out_specs=pl.BlockSpec((bsz, c, w), blk),
            out_shape=jax.ShapeDtypeStruct((bsz, total, w), F32),
            scratch_shapes=[pltpu.VMEM((bsz * N_HEADS, HEAD_DK, HEAD_DK), F32),
                            pltpu.VMEM((N_HEADS, c, c), F32),
                            pltpu.VMEM((c, w), F32), pltpu.VMEM((c, w), F32)],
            compiler_params=_params(1),
            name="retention_bwd" if reverse else "retention_fwd",
        )(*args)
    return y


def _lru_kernel(x_ref, xp_ref, xn_ref, cw_ref, cb_ref, gw_ref, gb_ref, lam_ref, *rest,
                reverse, ctx_len, total, n_ctx, n_all):
    if reverse:
        hf_ref, o_ref, buf_ref, carry_ref = rest
    else:
        o_ref, buf_ref, carry_ref = rest
    rows = LRU_BLOCK_ROWS
    s = pl.program_id(1)
    blk = _scan_block(s, n_ctx, n_all, reverse)

    @pl.when(s == 0)
    def _():
        carry_ref[...] = jnp.zeros_like(carry_ref)

    w = x_ref.shape[2] // 2
    _fill_halo(buf_ref, x_ref[0, :, 0:w], xp_ref[0], xn_ref[0], blk * rows, rows, ctx_len, total)
    xc = cb_ref[...] + sum(cw_ref[j:j + 1, :] * _shifted(buf_ref, j - CONV_K // 2, rows) for j in range(CONV_K))
    n_blocks = gw_ref.shape[0]
    bw = w // n_blocks
    parts = [_mm(xc[:, n * bw:(n + 1) * bw], gw_ref[n]) for n in range(n_blocks)]
    r_gate = _sigmoid(jnp.concatenate([p[:, :bw] for p in parts], axis=1) + gb_ref[0:1, :])
    i_gate = _sigmoid(jnp.concatenate([p[:, bw:] for p in parts], axis=1) + gb_ref[1:2, :])
    log_a = LRU_C * r_gate * (-_softplus(-lam_ref[...]))
    a = jnp.exp(log_a)
    b = jnp.sqrt(1.0 - jnp.exp(2.0 * log_a)) * i_gate * xc
    n_groups = rows // HALO
    a = a.reshape(n_groups, HALO, w)
    b = b.reshape(n_groups, HALO, w)
    rin = _iota2((1, HALO, 1), 1)
    d = 1
    while d < HALO:
        if reverse:
            a_s, b_s, ok = pltpu.roll(a, HALO - d, 1), pltpu.roll(b, HALO - d, 1), rin < HALO - d
        else:
            a_s, b_s, ok = pltpu.roll(a, d, 1), pltpu.roll(b, d, 1), rin >= d
        b = b + a * jnp.where(ok, b_s, 0.0)
        a = a * jnp.where(ok, a_s, 1.0)
        d *= 2
    carry = carry_ref[...]
    groups = [None] * n_groups
    for g in (range(n_groups - 1, -1, -1) if reverse else range(n_groups)):
        hg = b[g] + a[g] * carry
        carry = hg[0:1, :] if reverse else hg[HALO - 1:HALO, :]
        groups[g] = hg
    h = jnp.concatenate(groups, axis=0)
    carry_ref[...] = carry
    if reverse:
        h = (h + hf_ref[0]) * _gelu_tanh(x_ref[0, :, w:2 * w])
    o_ref[0] = h


def _rglru(p, conv_w, conv_b, gate_w, gate_b, lam, ctx_len):
    bsz, total, w2 = p.shape
    w = w2 // 2
    rows = LRU_BLOCK_ROWS
    n_all, n_ctx = total // rows, ctx_len // rows
    per = rows // HALO
    n8 = total // HALO
    n_blocks, bw = gate_w.shape[2], gate_w.shape[3]
    gw = jnp.transpose(gate_w, (0, 2, 3, 1, 4)).reshape(N_DIR, n_blocks, bw, 2 * bw).astype(BF16)
    h = None
    for reverse in (False, True):
        def blk(b, s, reverse=reverse):
            return (b, _scan_block(s, n_ctx, n_all, reverse), 0)

        def prev(b, s, reverse=reverse):
            return (b, jnp.maximum(_scan_block(s, n_ctx, n_all, reverse) * per - 1, 0), 0)

        def nxt(b, s, reverse=reverse):
            return (b, jnp.minimum((_scan_block(s, n_ctx, n_all, reverse) + 1) * per, n8 - 1), 0)

        d = int(reverse)
        in_specs = [pl.BlockSpec((1, rows, w2), blk),
                    pl.BlockSpec((1, HALO, w), prev),
                    pl.BlockSpec((1, HALO, w), nxt),
                    _const_spec((CONV_K, w)), _const_spec((1, w)),
                    _const_spec((n_blocks, bw, 2 * bw)), _const_spec((2, w)), _const_spec((1, w))]
        args = [p, p, p, conv_w, conv_b.reshape(1, w), gw[d], gate_b[d], lam[d].reshape(1, w)]
        if reverse:
            in_specs.append(pl.BlockSpec((1, rows, w), blk))
            args.append(h)
        h = pl.pallas_call(
            functools.partial(_lru_kernel, reverse=reverse, ctx_len=ctx_len, total=total,
                              n_ctx=n_ctx, n_all=n_all),
            grid=(bsz, n_all),
            in_specs=in_specs,
            out_specs=pl.BlockSpec((1, rows, w), blk),
            out_shape=jax.ShapeDtypeStruct((bsz, total, w), F32),
            scratch_shapes=[pltpu.VMEM((rows + 2 * HALO, w), F32), pltpu.VMEM((1, w), F32)],
            compiler_params=_params(),
            name="rglru_bwd" if reverse else "rglru_fwd",
        )(*args)
    return h


def _gdn_prepare(x_ref, xp_ref, xn_ref, abt_ref, cw_ref, alog_r_ref, dtb_r_ref, alog_c_ref, dtb_c_ref, buf_ref,
                 row0, *, reverse, ctx_len, total, with_gate):
    c = GDN_CHUNK
    w = N_HEADS * HEAD_DK
    bsz = x_ref.shape[0]
    ii = _iota2((c, c), 0)
    jj = _iota2((c, c), 1)
    incl = (ii <= jj) if reverse else (ii >= jj)
    incl_t = (ii >= jj) if reverse else (ii <= jj)
    last = 0 if reverse else c - 1
    ops = dict(kbq=[], k=[], vw=[], qe=[], kd=[], dec=[], egl=[], gate=[])
    for b in range(bsz):
        _fill_halo(buf_ref.at[b], x_ref[b, :, 0:3 * w], xp_ref[b], xn_ref[b], row0, c, ctx_len, total)
        yield None
        qkv = []
        for part in range(3):
            cols = slice(part * w, (part + 1) * w)
            qkv.append(_silu(sum(cw_ref[j:j + 1, cols]
                                 * buf_ref[b, HALO + j - CONV_K // 2:HALO + j - CONV_K // 2 + c, cols]
                                 for j in range(CONV_K))))
            yield None
        ab = x_ref[b, :, 4 * w:4 * w + LANES]
        abt = abt_ref[b]
        g_col = -jnp.exp(alog_r_ref[...]) * _softplus(ab + dtb_r_ref[...])
        g_row = -jnp.exp(alog_c_ref[...]) * _softplus(abt + dtb_c_ref[...])
        beta_col = _sigmoid(ab)
        gc_col = _tri_mm(jnp.where(incl, 1.0, 0.0), g_col)
        gc_row = _mm_tri(g_row, jnp.where(incl_t, 1.0, 0.0))
        yield None
        for h in range(N_HEADS):
            col = int(reverse) * N_HEADS + h
            gcc = gc_col[:, col:col + 1]
            gcr = gc_row[col:col + 1, :]
            beta = beta_col[:, 2 * N_HEADS + col:2 * N_HEADS + col + 1]
            ops["dec"].append(jnp.where(incl, jnp.exp(jnp.where(incl, gcc - gcr, 0.0)), 0.0))
            q = qkv[0][:, h * HEAD_DK:(h + 1) * HEAD_DK]
            k = qkv[1][:, h * HEAD_DK:(h + 1) * HEAD_DK]
            v = qkv[2][:, h * HEAD_DK:(h + 1) * HEAD_DK]
            q = q * lax.rsqrt(jnp.sum(q * q, axis=-1, keepdims=True) + EPS) * (HEAD_DK ** -0.5)
            k = k * lax.rsqrt(jnp.sum(k * k, axis=-1, keepdims=True) + EPS)
            e_gc = jnp.exp(gcc)
            g_last = gcc[last:last + 1, :]
            kb = k * beta
            ops["kbq"].append(jnp.concatenate([kb, q], axis=0).astype(BF16))
            ops["k"].append(k.astype(BF16))
            ops["vw"].append(jnp.concatenate([v * beta, kb * e_gc], axis=1).astype(BF16))
            ops["qe"].append((q * e_gc).astype(BF16))
            ops["kd"].append((k * jnp.exp(g_last - gcc)).astype(BF16))
            ops["egl"].append(jnp.broadcast_to(jnp.exp(g_last), (1, HEAD_DK)))
            yield None
        if with_gate:
            ops["gate"].append(_silu(x_ref[b, :, 3 * w:4 * w]))
            yield None
    yield ops


def _gdn_kernel(x_ref, xp_ref, xn_ref, abt_ref, cw_ref, alog_r_ref, dtb_r_ref, alog_c_ref, dtb_c_ref,
                ng_ref, *rest, reverse, ctx_len, total, n_ctx, n_all):
    if reverse:
        yf_ref, o_ref = rest[:2]
        rest = rest[2:]
    else:
        o_ref = rest[0]
        rest = rest[1:]
    buf_ref, s_ref, kbq_ref, k_ref, vw_ref, qe_ref, kd_ref, dec_ref, egl_ref = rest[:9]
    gate_ref = rest[9] if reverse else None
    stash = dict(kbq=kbq_ref, k=k_ref, vw=vw_ref, qe=qe_ref, kd=kd_ref, dec=dec_ref, egl=egl_ref, gate=gate_ref)
    c = GDN_CHUNK
    s = pl.program_id(0)
    bsz = x_ref.shape[0]

    @pl.when(s == 0)
    def _():
        s_ref[...] = jnp.zeros_like(s_ref)
        for ref in stash.values():
            if ref is not None:
                ref[...] = jnp.zeros_like(ref)

    wr = s % 2
    rd = 1 - wr
    row0 = _scan_block(jnp.minimum(s, n_all - 1), n_ctx, n_all, reverse) * c
    prep = _gdn_prepare(x_ref, xp_ref, xn_ref, abt_ref, cw_ref, alog_r_ref, dtb_r_ref, alog_c_ref, dtb_c_ref,
                        buf_ref, row0, reverse=reverse, ctx_len=ctx_len, total=total, with_gate=reverse)
    new_ops = []

    def advance(count=1):
        for _ in range(count):
            got = next(prep, None)
            if got is not None:
                new_ops.append(got)

    ii = _iota2((c, c), 0)
    jj = _iota2((c, c), 1)
    strict = (ii < jj) if reverse else (ii > jj)
    chains = [(b, h) for b in range(bsz) for h in range(N_HEADS)]
    n = len(chains)
    states = [s_ref[i] for i in range(n)]
    decs = [dec_ref[rd, i] for i in range(n)]
    kq = [lax.dot_general(kbq_ref[rd, i], k_ref[rd, i], (((1,), (1,)), ((), ())), preferred_element_type=F32)
          for i in range(n)]
    advance(2)
    t_invs = _tri_inverse([jnp.where(strict, m[0:c] * dec, 0.0) for m, dec in zip(kq, decs)], c, advance)
    qks = [m[c:2 * c] * dec for m, dec in zip(kq, decs)]
    uw = [jnp.dot(t.astype(BF16), vw_ref[rd, i], preferred_element_type=F32) for i, t in enumerate(t_invs)]
    advance(2)
    ws = [jnp.dot(jnp.concatenate([m[:, HEAD_DK:].astype(BF16), qe_ref[rd, i]], axis=0), st.astype(BF16),
                  preferred_element_type=F32) for i, (m, st) in enumerate(zip(uw, states))]
    advance(2)
    v_news = [(m[:, :HEAD_DK] - x[0:c]).astype(BF16) for m, x in zip(uw, ws)]
    ys = [x[c:2 * c] + jnp.dot(qk.astype(BF16), vn, preferred_element_type=F32)
          for x, qk, vn in zip(ws, qks, v_news)]
    advance(2)
    new_states = [st * egl_ref[rd, i] + lax.dot_general(kd_ref[rd, i], vn, (((0,), (0,)), ((), ())),
                                                     preferred_element_type=F32)
                  for i, (st, vn) in enumerate(zip(states, v_news))]
    advance(2)
    for i, (b, h) in enumerate(chains):
        s_ref[i] = new_states[i]
        y = ys[i]
        if reverse:
            y = y + yf_ref[b, :, h * HEAD_DK:(h + 1) * HEAD_DK]
            y = (y * lax.rsqrt(jnp.mean(y * y, axis=-1, keepdims=True) + EPS) * ng_ref[...]
                 * gate_ref[rd, b, :, h * HEAD_DK:(h + 1) * HEAD_DK])
        o_ref[b, :, h * HEAD_DK:(h + 1) * HEAD_DK] = y
    advance(64)
    (ops,) = new_ops
    for name, values in ops.items():
        for i, value in enumerate(values):
            stash[name][wr, i] = value


def _gated_deltanet(p, abt, conv_w, a_log, dt_bias, norm_g, ctx_len):
    bsz, total, pw = p.shape
    w = N_HEADS * HEAD_DK
    c = GDN_CHUNK
    n_all, n_ctx = total // c, ctx_len // c
    per = c // HALO
    n8 = total // HALO
    nt = abt.shape[1]
    flat_alog = a_log.reshape(-1)
    flat_dtb = dt_bias.reshape(-1)
    alog_r = jnp.zeros((1, LANES), F32).at[0, :flat_alog.shape[0]].set(flat_alog)
    dtb_r = jnp.zeros((1, LANES), F32).at[0, :flat_dtb.shape[0]].set(flat_dtb)
    alog_c = jnp.zeros((nt, 1), F32).at[:flat_alog.shape[0], 0].set(flat_alog)
    dtb_c = jnp.zeros((nt, 1), F32).at[:flat_dtb.shape[0], 0].set(flat_dtb)
    y = None
    n_chains = bsz * N_HEADS
    for reverse in (False, True):
        def src(s, reverse=reverse):
            return _scan_block(jnp.minimum(s, n_all - 1), n_ctx, n_all, reverse)

        def blk(s, src=src):
            return (0, src(s), 0)

        def blk_t(s, src=src):
            return (0, 0, src(s))

        def prev(s, src=src):
            return (0, jnp.maximum(src(s) * per - 1, 0), 0)

        def nxt(s, src=src):
            return (0, jnp.minimum((src(s) + 1) * per, n8 - 1), 0)

        def dst(s, reverse=reverse):
            return (0, _scan_block(jnp.maximum(s - 1, 0), n_ctx, n_all, reverse), 0)

        def out_dst(s, dst=dst):
            return (0, jnp.where(s == 0, n_all, dst(s)[1]), 0)

        in_specs = [pl.BlockSpec((bsz, c, pw), blk),
                    pl.BlockSpec((bsz, HALO, 3 * w), prev),
                    pl.BlockSpec((bsz, HALO, 3 * w), nxt),
                    pl.BlockSpec((bsz, nt, c), blk_t),
                    _const_spec((CONV_K, 3 * w)),
                    _const_spec((1, LANES)), _const_spec((1, LANES)),
                    _const_spec((nt, 1)), _const_spec((nt, 1)),
                    _const_spec((1, HEAD_DK))]
        args = [p, p, p, abt, conv_w, alog_r, dtb_r, alog_c, dtb_c, norm_g.reshape(1, HEAD_DK)]
        scratch = [pltpu.VMEM((bsz, c + 2 * HALO, 3 * w), F32),
                   pltpu.VMEM((n_chains, HEAD_DK, HEAD_DK), F32),
                   pltpu.VMEM((2, n_chains, 2 * c, HEAD_DK), BF16),
                   pltpu.VMEM((2, n_chains, c, HEAD_DK), BF16),
                   pltpu.VMEM((2, n_chains, c, 2 * HEAD_DK), BF16),
                   pltpu.VMEM((2, n_chains, c, HEAD_DK), BF16),
                   pltpu.VMEM((2, n_chains, c, HEAD_DK), BF16),
                   pltpu.VMEM((2, n_chains, c, c), F32),
                   pltpu.VMEM((2, n_chains, 1, HEAD_DK), F32)]
        if reverse:
            in_specs.append(pl.BlockSpec((bsz, c, w), dst))
            args.append(y)
            scratch.append(pltpu.VMEM((2, bsz, c, w), F32))
        y = pl.pallas_call(
            functools.partial(_gdn_kernel, reverse=reverse, ctx_len=ctx_len, total=total,
                              n_ctx=n_ctx, n_all=n_all),
            grid=(n_all + 1,),
            in_specs=in_specs,
            out_specs=pl.BlockSpec((bsz, c, w), out_dst),
            out_shape=jax.ShapeDtypeStruct((bsz, total + c, w), F32),
            scratch_shapes=scratch,
            compiler_params=_params(1),
            name="gdn_bwd" if reverse else "gdn_fwd",
        )(*args)
    return y


def _rwkv_prepare(x_ref, xp_ref, xn_ref, mu_ref, w0_ref, w2_ref, a0_ref, a2_ref, kk_ref, ka_ref, hs_ref,
                  fin_refs, buf_ref, row0, n_pairs, *, reverse, ctx_len, total):
    c = RWKV_CHUNK
    bsz = x_ref.shape[0]
    bw = n_pairs * LANES
    rows = []
    for b in range(bsz):
        xb = x_ref[b]
        _fill_halo(buf_ref.at[b], xb, xp_ref[b], xn_ref[b], row0, c, ctx_len, total)
        rows.append(xb + (0.5 * (_shifted(buf_ref.at[b], -1, c) + _shifted(buf_ref.at[b], 1, c)) - xb) * mu_ref[...])
        yield None
    x = jnp.concatenate(rows, axis=0)
    n = bsz * c
    r = x[:, 0:bw]
    k = x[:, bw:2 * bw]
    v = x[:, 2 * bw:3 * bw]
    off = 3 * bw
    gd = x[:, off:off + RWKV_GATE_RANK]
    off += RWKV_GATE_RANK
    wd = [x[:, off + d * RWKV_RANK:off + (d + 1) * RWKV_RANK] for d in range(N_DIR)]
    off += N_DIR * RWKV_RANK
    ad = [x[:, off + d * RWKV_RANK:off + (d + 1) * RWKV_RANK] for d in range(N_DIR)]

    def per_head_sum(t):
        t1, t2, t3 = _split3(t)
        hs = hs_ref[...]
        return (jnp.dot(t1, hs, preferred_element_type=F32) + jnp.dot(t2, hs, preferred_element_type=F32)
                + jnp.dot(t3, hs, preferred_element_type=F32))

    def a_of(d):
        return _sigmoid(a0_ref[d:d + 1, :] + _mm(ad[d], a2_ref[d]))

    d = int(reverse)
    w_raw = -_softplus(-(w0_ref[d:d + 1, :] + _mm(jnp.tanh(wd[d]), w2_ref[d]))) - 0.5
    logw = -jnp.exp(w_raw)
    yield None
    a_dir = a_of(d)
    kd = k * (1.0 + (a_dir - 1.0) * ka_ref[...])
    yield None
    kkv = k * kk_ref[...]
    kkv = kkv * lax.rsqrt(per_head_sum(kkv * kkv) + EPS)
    yield None

    cshift = c.bit_length() - 1
    ii = _iota2((n, n), 0)
    jj = _iota2((n, n), 1)
    incl = ((ii <= jj) if reverse else (ii >= jj)) & ((ii >> cshift) == (jj >> cshift))
    cum = _tri_mm(jnp.where(incl, 1.0, 0.0), logw)
    last = 0 if reverse else c - 1
    cum_tot = jnp.concatenate([jnp.broadcast_to(cum[b * c + last:b * c + last + 1, :], (c, bw))
                               for b in range(bsz)], axis=0)
    yield None
    w_inv = jnp.exp(-cum)
    w_rest = jnp.exp(cum_tot - cum)
    beta = kkv * a_dir
    alpha_t = -kkv * jnp.exp(cum - logw)
    r_t = r * jnp.exp(cum)
    yield None
    beta_t = beta * w_inv
    k_t = kd * w_inv
    yield None
    beta_h = beta * w_rest
    k_h = kd * w_rest
    yield None

    lane = _iota2((1, LANES), 1)
    m0 = jnp.where(lane < RWKV_N, 1.0, 0.0)
    m1 = 1.0 - m0

    def stack(parts, b, p):
        tiles = [t[b * c:(b + 1) * c, p * LANES:(p + 1) * LANES] for t in parts]
        return jnp.concatenate([(t * m).astype(BF16) for t in tiles for m in (m0, m1)], axis=0)

    ops = dict(ar=[], bk=[], vs=[], bkh=[], wtot=[], fin=[])
    for b in range(bsz):
        for p in range(n_pairs):
            ops["ar"].append(stack([alpha_t, r_t], b, p))
            ops["bk"].append(stack([beta_t, k_t], b, p))
            ops["vs"].append(stack([v], b, p))
            ops["bkh"].append(stack([beta_h, k_h], b, p))
            ops["wtot"].append(jnp.exp(cum_tot[b * c:b * c + 1, p * LANES:(p + 1) * LANES]))
            yield None
    if fin_refs is not None:
        g2_ref, rk_ref = fin_refs
        k_sum = kd + k * (1.0 + (a_of(0) - 1.0) * ka_ref[...])
        yield None
        ops["fin"].append(per_head_sum(r * k_sum * rk_ref[...]) * v)
        yield None
        ops["fin"].append(_mm(_sigmoid(gd), g2_ref[...]))
        yield None
    yield ops


def _rwkv_kernel(x_ref, xp_ref, xn_ref, mu_ref, w0_ref, w2_ref, a0_ref, a2_ref, kk_ref, ka_ref, hs_ref, *rest,
                 reverse, ctx_len, total, n_ctx, n_all):
    if reverse:
        g2_ref, rk_ref, lng_ref, lnb_ref, yf_ref, o_ref = rest[:6]
        rest = rest[6:]
    else:
        o_ref = rest[0]
        rest = rest[1:]
    buf_ref, s_ref, ar_ref, bk_ref, vs_ref, bkh_ref, wtot_ref = rest[:7]
    fin_ref = rest[7] if reverse else None
    stash = dict(ar=ar_ref, bk=bk_ref, vs=vs_ref, bkh=bkh_ref, wtot=wtot_ref, fin=fin_ref)
    c = RWKV_CHUNK
    step = pl.program_id(0)
    bsz = x_ref.shape[0]
    n_chains = s_ref.shape[0]
    n_pairs = n_chains // bsz

    @pl.when(step == 0)
    def _():
        s_ref[...] = jnp.zeros_like(s_ref)
        for ref in stash.values():
            if ref is not None:
                ref[...] = jnp.zeros_like(ref)

    wr = step % 2
    rd = 1 - wr
    row0 = _scan_block(jnp.minimum(step, n_all - 1), n_ctx, n_all, reverse) * c
    prep = _rwkv_prepare(x_ref, xp_ref, xn_ref, mu_ref, w0_ref, w2_ref, a0_ref, a2_ref, kk_ref, ka_ref, hs_ref,
                         (g2_ref, rk_ref) if reverse else None, buf_ref, row0, n_pairs,
                         reverse=reverse, ctx_len=ctx_len, total=total)
    new_ops = []

    def advance(count=1):
        for _ in range(count):
            got = next(prep, None)
            if got is not None:
                new_ops.append(got)

    n2 = 2 * c
    cshift = c.bit_length() - 1
    ii = _iota2((n2, n2), 0)
    jj = _iota2((n2, n2), 1)
    same = (ii >> cshift) == (jj >> cshift)
    incl = ((ii <= jj) if reverse else (ii >= jj)) & same
    strict = ((ii < jj) if reverse else (ii > jj)) & same
    incl2 = jnp.concatenate([incl, incl], axis=1)
    nt_dims = (((1,), (1,)), ((), ()))

    states = [s_ref[i] for i in range(n_chains)]
    g = [lax.dot_general(ar_ref[rd, i], bk_ref[rd, i], nt_dims, preferred_element_type=F32)
         for i in range(n_chains)]
    advance(2)
    t_invs = _tri_inverse([jnp.where(strict, -m[0:n2, 0:n2], 0.0) for m in g], c, advance)
    mv = [jnp.dot(jnp.where(strict, m[0:n2, n2:2 * n2], 0.0).astype(BF16), vs_ref[rd, i], preferred_element_type=F32)
          for i, m in enumerate(g)]
    advance(2)
    xs = [lax.dot_general(ar_ref[rd, i], st.astype(BF16), nt_dims, preferred_element_type=F32)
          for i, st in enumerate(states)]
    advance(2)
    us = [_mm(t, x1[0:n2] + m) for t, x1, m in zip(t_invs, xs, mv)]
    advance(2)
    uv = [jnp.concatenate([u.astype(BF16), vs_ref[rd, i]], axis=0) for i, u in enumerate(us)]
    ys = [x1[n2:2 * n2] + jnp.dot(jnp.where(incl2, m[n2:2 * n2, :], 0.0).astype(BF16), w,
                                  preferred_element_type=F32) for x1, m, w in zip(xs, g, uv)]
    advance(2)
    for i in range(n_chains):
        s_ref[i] = states[i] * wtot_ref[rd, i] + lax.dot_general(uv[i], bkh_ref[rd, i], (((0,), (0,)), ((), ())),
                                                              preferred_element_type=F32)
    advance(2)
    y = jnp.concatenate([jnp.concatenate([ys[b * n_pairs + p][0:c] + ys[b * n_pairs + p][c:n2]
                                          for p in range(n_pairs)], axis=1) for b in range(bsz)], axis=0)
    if reverse:
        hs = hs_ref[...]

        def per_head_sum(t):
            t1, t2, t3 = _split3(t)
            return (jnp.dot(t1, hs, preferred_element_type=F32) + jnp.dot(t2, hs, preferred_element_type=F32)
                    + jnp.dot(t3, hs, preferred_element_type=F32))

        y = y + jnp.concatenate([yf_ref[b] for b in range(bsz)], axis=0)
        mean = per_head_sum(y) * (1.0 / RWKV_N)
        yc = y - mean
        var = per_head_sum(yc * yc) * (1.0 / RWKV_N)
        yn = yc * lax.rsqrt(var + RWKV_LN_EPS) * lng_ref[...] + lnb_ref[...]
        y = (yn + fin_ref[rd, 0]) * fin_ref[rd, 1]
    for b in range(bsz):
        o_ref[b] = y[b * c:(b + 1) * c]
    advance(64)
    (ops,) = new_ops
    for name, values in ops.items():
        for i, value in enumerate(values):
            stash[name][wr, i] = value


def _rwkv7(p, mu, w0, w2, a0, a2, g2, k_k, k_a, r_k, ln_g, ln_b, ctx_len):
    bsz, total, pw = p.shape
    bw = k_k.shape[0]
    c = RWKV_CHUNK
    n_all, n_ctx = total // c, ctx_len // c
    per = c // HALO
    n8 = total // HALO
    head_id = jnp.arange(bw) // RWKV_N
    head_sum = (head_id[:, None] == head_id[None, :]).astype(BF16)
    y = None
    n_chains = bsz * (bw // LANES)
    for reverse in (False, True):
        def src(s, reverse=reverse):
            return _scan_block(jnp.minimum(s, n_all - 1), n_ctx, n_all, reverse)

        def blk(s, src=src):
            return (0, src(s), 0)

        def prev(s, src=src):
            return (0, jnp.maximum(src(s) * per - 1, 0), 0)

        def nxt(s, src=src):
            return (0, jnp.minimum((src(s) + 1) * per, n8 - 1), 0)

        def dst(s, reverse=reverse):
            return (0, _scan_block(jnp.maximum(s - 1, 0), n_ctx, n_all, reverse), 0)

        def out_dst(s, dst=dst):
            return (0, jnp.where(s == 0, n_all, dst(s)[1]), 0)

        in_specs = [pl.BlockSpec((bsz, c, pw), blk),
                    pl.BlockSpec((bsz, HALO, pw), prev),
                    pl.BlockSpec((bsz, HALO, pw), nxt),
                    _const_spec((1, pw)),
                    _const_spec((N_DIR, bw)), _const_spec((N_DIR, RWKV_RANK, bw)),
                    _const_spec((N_DIR, bw)), _const_spec((N_DIR, RWKV_RANK, bw)),
                    _const_spec((1, bw)), _const_spec((1, bw)), _const_spec((bw, bw))]
        args = [p, p, p, mu.reshape(1, pw), w0, w2.astype(BF16), a0, a2.astype(BF16),
                k_k.reshape(1, bw), k_a.reshape(1, bw), head_sum]
        scratch = [pltpu.VMEM((bsz, c + 2 * HALO, pw), F32),
                   pltpu.VMEM((n_chains, LANES, LANES), F32),
                   pltpu.VMEM((2, n_chains, 4 * c, LANES), BF16),
                   pltpu.VMEM((2, n_chains, 4 * c, LANES), BF16),
                   pltpu.VMEM((2, n_chains, 2 * c, LANES), BF16),
                   pltpu.VMEM((2, n_chains, 4 * c, LANES), BF16),
                   pltpu.VMEM((2, n_chains, 1, LANES), F32)]
        if reverse:
            in_specs += [_const_spec((RWKV_GATE_RANK, bw)), _const_spec((1, bw)), _const_spec((1, bw)),
                         _const_spec((1, bw)), pl.BlockSpec((bsz, c, bw), dst)]
            args += [g2.astype(BF16), r_k.reshape(1, bw), ln_g.reshape(1, bw), ln_b.reshape(1, bw), y]
            scratch.append(pltpu.VMEM((2, 2, bsz * c, bw), F32))
        y = pl.pallas_call(
            functools.partial(_rwkv_kernel, reverse=reverse, ctx_len=ctx_len, total=total,
                              n_ctx=n_ctx, n_all=n_all),
            grid=(n_all + 1,),
            in_specs=in_specs,
            out_specs=pl.BlockSpec((bsz, c, bw), out_dst),
            out_shape=jax.ShapeDtypeStruct((bsz, total + c, bw), F32),
            scratch_shapes=scratch,
            compiler_params=_params(1),
            name="rwkv_bwd" if reverse else "rwkv_fwd",
        )(*args)
    return y


def _merge_kernel(h_ref, xm_ref, mod_ref, wg_ref, gb_ref, y0_ref, y1_ref, y2_ref, y3_ref, bw_ref, ow_ref,
                  o_ref, *, rows, ctx_len):
    tile = pl.program_id(1)
    h = h_ref[0]
    d = h.shape[1]
    xm = xm_ref[0]
    merged = jnp.zeros_like(h)
    for k, y_ref in enumerate((y0_ref, y1_ref, y2_ref, y3_ref)):
        gate = _sigmoid(jnp.dot(xm, wg_ref[:, k * d:(k + 1) * d], preferred_element_type=F32)
                        + gb_ref[:, k * d:(k + 1) * d])
        merged = merged + gate * _mm(y_ref[0], bw_ref[k])
    o_ref[0] = h + _mod_row(mod_ref, tile, rows, ctx_len, 2) * _mm(merged, ow_ref[...])


def _merge(h, xm, modtab, w_gates, gate_b, ys, branch_w, out_w, ctx_len):
    bsz, total, d = h.shape
    rows = _token_tile(total, TOKEN_TILES)
    bwid = ys[0].shape[2]
    row_spec = lambda width: pl.BlockSpec((1, rows, width), lambda b, i: (b, i, 0))
    return pl.pallas_call(
        functools.partial(_merge_kernel, rows=rows, ctx_len=ctx_len),
        grid=(bsz, total // rows),
        in_specs=[row_spec(d), row_spec(d),
                  pl.BlockSpec((1, 2, 6, d), lambda b, i: (b, 0, 0, 0)),
                  _weight_spec((d, N_BRANCH * d)), _const_spec((1, N_BRANCH * d)),
                  row_spec(bwid), row_spec(bwid), row_spec(bwid), row_spec(bwid),
                  _weight_spec((N_BRANCH, bwid, d)), _weight_spec((d, d))],
        out_specs=row_spec(d),
        out_shape=jax.ShapeDtypeStruct((bsz, total, d), F32),
        compiler_params=_params(),
        name="merge",
    )(h, xm, modtab, w_gates, gate_b.reshape(1, -1), *ys, branch_w, out_w)


def _ffn_kernel(h_ref, g_ref, mod_ref, w1_ref, w3_ref, w2_ref, fg_ref, o_ref, *, rows, ctx_len, final):
    tile = pl.program_id(1)
    h = h_ref[0]
    u = _norm_modulate(h, g_ref[...], mod_ref, tile, rows, ctx_len, 3, 4).astype(BF16)
    dff = w1_ref.shape[1]
    part = dff // FFN_SPLIT
    out = jnp.zeros_like(h)
    for j in range(FFN_SPLIT):
        cols = slice(j * part, (j + 1) * part)
        t = (_silu(jnp.dot(u, w1_ref[:, cols], preferred_element_type=F32))
             * jnp.dot(u, w3_ref[:, cols], preferred_element_type=F32))
        out = out + _mm(t, w2_ref[cols, :])
    h = h + _mod_row(mod_ref, tile, rows, ctx_len, 5) * out
    if final:
        h = h * lax.rsqrt(jnp.mean(h * h, axis=-1, keepdims=True) + EPS) * fg_ref[...]
    o_ref[0] = h


def _ffn(h, g, modtab, w1, w3, w2, final_g, ctx_len, final):
    bsz, total, d = h.shape
    rows = _token_tile(total, TOKEN_TILES)
    dff = w1.shape[1]
    row_spec = pl.BlockSpec((1, rows, d), lambda b, i: (b, i, 0))
    return pl.pallas_call(
        functools.partial(_ffn_kernel, rows=rows, ctx_len=ctx_len, final=final),
        grid=(bsz, total // rows),
        in_specs=[row_spec, _const_spec((1, d)),
                  pl.BlockSpec((1, 2, 6, d), lambda b, i: (b, 0, 0, 0)),
                  _weight_spec((d, dff)), _weight_spec((d, dff)), _weight_spec((dff, d)), _const_spec((1, d))],
        out_specs=row_spec,
        out_shape=jax.ShapeDtypeStruct((bsz, total, d), F32),
        compiler_params=_params(),
        name="ffn",
    )(h, g.reshape(1, d), modtab, w1, w3, w2, final_g.reshape(1, d))


def _rope_tables(seq, ctx_len):
    t = jnp.arange(seq)
    row = (t // GRID_W).astype(F32)
    col = (t % GRID_W).astype(F32)
    quarter = HEAD_DK // 4
    inv = ROPE_BASE ** (-jnp.arange(quarter, dtype=F32) / quarter)
    ang = jnp.concatenate([row[:, None] * inv, col[:, None] * inv], axis=-1)
    cos, sin = jnp.cos(ang), jnp.sin(ang)
    cos = jnp.concatenate([cos, cos], axis=-1)
    sin = jnp.concatenate([-sin, sin], axis=-1)
    cos = jnp.concatenate([jnp.ones((ctx_len, HEAD_DK), F32), cos], axis=0)
    sin = jnp.concatenate([jnp.zeros((ctx_len, HEAD_DK), F32), sin], axis=0)
    return cos, sin


def kernel(x, c, ctx, c_ctx, mod_w, mod_b, norm1_g, norm2_g, in_w, gate_b, ret_decay_exp, lru_conv_w,
           lru_conv_b, lru_gate_w, lru_gate_b, lru_lambda, gdn_conv_w, gdn_a_log, gdn_dt_bias, gdn_norm_g,
           rwkv_mu, rwkv_w0, rwkv_w2, rwkv_a0, rwkv_a2, rwkv_g2, rwkv_k_k, rwkv_k_a, rwkv_r_k, rwkv_ln_g,
           rwkv_ln_b, branch_w, out_w, ffn_w1, ffn_w3, ffn_w2, final_norm_g):
    bsz, seq, d = x.shape
    ctx_len = ctx.shape[1]
    depth = mod_w.shape[0]
    bw = d // 2
    assert ctx_len % LRU_BLOCK_ROWS == 0 and seq % LRU_BLOCK_ROWS == 0 and bsz + 1 <= HALO

    h = jnp.concatenate([ctx, x], axis=1)
    cond = jnp.zeros((HALO, d), F32).at[:bsz].set(c).at[bsz].set(c_ctx)
    mods = _modulation(cond, mod_w, mod_b).reshape(depth, HALO, 6, d)
    cos, sin = _rope_tables(seq, ctx_len)

    widths = (4 * bw, 2 * bw, 3 * bw, bw, N_DIR * N_HEADS, N_DIR * N_HEADS,
              3 * bw + RWKV_GATE_RANK + 2 * N_DIR * RWKV_RANK, N_BRANCH * d)
    offs = [0]
    for wdt in widths:
        offs.append(offs[-1] + wdt)

    for l in range(depth):
        lat = mods[l, :bsz]
        modtab = jnp.stack([jnp.broadcast_to(mods[l, bsz], lat.shape), lat], axis=1)
        wl = in_w[l]
        w_ret = wl[:, offs[0]:offs[1]].astype(BF16)
        w_lru = wl[:, offs[1]:offs[2]].astype(BF16)
        w_ab = wl[:, offs[4]:offs[6]]
        w_gdn = jnp.concatenate([wl[:, offs[2]:offs[4]], w_ab,
                                 jnp.zeros((d, LANES - w_ab.shape[1]), F32)], axis=1).astype(BF16)
        w_abt = w_ab.T.astype(BF16)
        w_rwkv = wl[:, offs[6]:offs[7]].astype(BF16)
        w_gates = wl[:, offs[7]:offs[8]].astype(BF16)

        xm = _mixer_input(h, norm1_g[l], modtab, ctx_len)
        p_ret = _in_projection(xm, w_ret)
        p_lru = _in_projection(xm, w_lru)
        p_gdn, abt = _in_projection(xm, w_gdn, w_t=w_abt)
        p_rwkv = _in_projection(xm, w_rwkv)

        log_gamma = jnp.log1p(-jnp.exp2(-ret_decay_exp[l].astype(F32)))
        lg_rows = jnp.repeat(log_gamma, HEAD_DK, axis=1).reshape(N_DIR, 1, N_HEADS * HEAD_DK)
        y_ret = _retention(p_ret, cos, sin, lg_rows, ctx_len)
        y_lru = _rglru(p_lru, lru_conv_w[l], lru_conv_b[l], lru_gate_w[l], lru_gate_b[l], lru_lambda[l], ctx_len)
        y_gdn = _gated_deltanet(p_gdn, abt, gdn_conv_w[l], gdn_a_log[l], gdn_dt_bias[l], gdn_norm_g[l], ctx_len)
        y_rwkv = _rwkv7(p_rwkv, rwkv_mu[l], rwkv_w0[l], rwkv_w2[l], rwkv_a0[l], rwkv_a2[l], rwkv_g2[l],
                        rwkv_k_k[l], rwkv_k_a[l], rwkv_r_k[l].reshape(-1), rwkv_ln_g[l], rwkv_ln_b[l], ctx_len)

        h = _merge(h, xm, modtab, w_gates, gate_b[l], (y_ret, y_lru, y_gdn, y_rwkv),
                   branch_w[l].astype(BF16), out_w[l].astype(BF16), ctx_len)
        h = _ffn(h, norm2_g[l], modtab, ffn_w1[l].astype(BF16), ffn_w3[l].astype(BF16),
                 ffn_w2[l].astype(BF16), final_norm_g, ctx_len, final=(l == depth - 1))
    return h[:, ctx_len:]
```

```python
import functools

import jax
import jax.numpy as jnp
from jax import lax
from jax.experimental import pallas as pl
from jax.experimental.pallas import tpu as pltpu

F32 = jnp.float32
BF16 = jnp.bfloat16

GRID_W = 64
ROPE_BASE = 10000.0
EPS = 1e-6
RWKV_LN_EPS = 64e-5
LRU_C = 8.0
N_DIR = 2
N_BRANCH = 4
HEAD_DK = 128
N_HEADS = 4
RWKV_N = 64
RWKV_RANK = 64
RWKV_GATE_RANK = 128
CONV_K = 4
LANES = 128
HALO = 8
VMEM_LIMIT_BYTES = 56 * 1024 * 1024

TOKEN_TILES = (768, 512, 256)
PROJ_TILES = (1408,) + TOKEN_TILES
RET_CHUNK = 128
LRU_BLOCK_ROWS = 256
GDN_CHUNK = 128
RWKV_CHUNK = 64
FFN_TILES = (256,)
FFN_SPLIT = 1


def _mm(a, b):
    return jnp.dot(a.astype(BF16), b.astype(BF16), preferred_element_type=F32)


def _mm_nt(a, b):
    return lax.dot_general(a.astype(BF16), b.astype(BF16), (((1,), (1,)), ((), ())),
                           preferred_element_type=F32)


def _mm_tn(a, b):
    return lax.dot_general(a.astype(BF16), b.astype(BF16), (((0,), (0,)), ((), ())),
                           preferred_element_type=F32)


def _split3(x):
    x1 = x.astype(BF16)
    r1 = x - x1.astype(F32)
    x2 = r1.astype(BF16)
    x3 = (r1 - x2.astype(F32)).astype(BF16)
    return x1, x2, x3


def _tri_mm(tri, x):
    x1, x2, x3 = _split3(x)
    t = tri.astype(BF16)
    return (jnp.dot(t, x1, preferred_element_type=F32) + jnp.dot(t, x2, preferred_element_type=F32)
            + jnp.dot(t, x3, preferred_element_type=F32))


def _mm_tri(x, tri):
    x1, x2, x3 = _split3(x)
    t = tri.astype(BF16)
    return (jnp.dot(x1, t, preferred_element_type=F32) + jnp.dot(x2, t, preferred_element_type=F32)
            + jnp.dot(x3, t, preferred_element_type=F32))


def _sigmoid(x):
    return 1.0 / (1.0 + jnp.exp(-x))


def _silu(x):
    return x * _sigmoid(x)


def _softplus(x):
    return jnp.maximum(x, 0.0) + jnp.log1p(jnp.exp(-jnp.abs(x)))


def _gelu_tanh(x):
    return 0.5 * x * (1.0 + jnp.tanh(0.7978845608028654 * (x + 0.044715 * x * x * x)))


def _iota2(shape, axis):
    return lax.broadcasted_iota(jnp.int32, shape, axis)


def _scan_block(s, n_ctx, n_all, reverse):
    if not reverse:
        return s
    return jnp.where(s < n_ctx, n_ctx - 1 - s, n_all - 1 - (s - n_ctx))


def _tri_inverse(mats, n_max):
    n = mats[0].shape[0]
    ii = _iota2((n, n), 0)
    jj = _iota2((n, n), 1)
    eye = jnp.where(ii == jj, 1.0, 0.0)
    pair = (ii >> 1) == (jj >> 1)
    ts = [eye - jnp.where(pair, a, 0.0) for a in mats]
    b = 2
    while b < n_max:
        sh = b.bit_length() - 1
        between = ((ii >> (sh + 1)) == (jj >> (sh + 1))) & ((ii >> sh) != (jj >> sh))
        xs = [_mm(t, jnp.where(between, a, 0.0)) for t, a in zip(ts, mats)]
        ts = [t - _mm(x, t) for t, x in zip(ts, xs)]
        b *= 2
    return ts


def _fill_halo(buf_ref, x, prev, nxt, row0, rows, ctx_len, total):
    prev_ok = jnp.logical_and(row0 != 0, row0 != ctx_len)
    next_ok = jnp.logical_and(row0 + rows != ctx_len, row0 + rows != total)
    buf_ref[0:HALO, :] = jnp.where(prev_ok, prev, 0.0)
    buf_ref[HALO:HALO + rows, :] = x
    buf_ref[HALO + rows:HALO + rows + HALO, :] = jnp.where(next_ok, nxt, 0.0)


def _shifted(buf_ref, off, rows):
    return buf_ref[HALO + off:HALO + off + rows, :]


def _norm_modulate(x, g, mod_ref, tile, rows, ctx_len, shift_idx, scale_idx):
    xn = x * lax.rsqrt(jnp.mean(x * x, axis=-1, keepdims=True) + EPS) * g
    is_ctx = (tile * rows + _iota2((rows, 1), 0)) < ctx_len
    shift = jnp.where(is_ctx, mod_ref[0, 0, shift_idx:shift_idx + 1, :], mod_ref[0, 1, shift_idx:shift_idx + 1, :])
    scale = jnp.where(is_ctx, mod_ref[0, 0, scale_idx:scale_idx + 1, :], mod_ref[0, 1, scale_idx:scale_idx + 1, :])
    return xn * (1.0 + scale) + shift


def _mod_row(mod_ref, tile, rows, ctx_len, idx):
    is_ctx = (tile * rows + _iota2((rows, 1), 0)) < ctx_len
    return jnp.where(is_ctx, mod_ref[0, 0, idx:idx + 1, :], mod_ref[0, 1, idx:idx + 1, :])


def _params(n_axes=2):
    return pltpu.CompilerParams(dimension_semantics=("arbitrary",) * n_axes,
                                vmem_limit_bytes=VMEM_LIMIT_BYTES)


def _const_spec(shape):
    nd = len(shape)
    return pl.BlockSpec(shape, lambda *_: (0,) * nd)


def _weight_spec(shape):
    nd = len(shape)
    return pl.BlockSpec(shape, lambda *_: (0,) * nd, pipeline_mode=pl.Buffered(1))


def _token_tile(total, candidates):
    for rows in candidates:
        if total % rows == 0:
            return rows
    raise ValueError(f"token count {total} is not a multiple of {candidates[-1]}")


def _mod_kernel(c_ref, w_ref, b_ref, o_ref):
    o_ref[0] = _mm(_silu(c_ref[...]), w_ref[0]) + b_ref[0]


def _modulation(cond, mod_w, mod_b):
    depth, d, n = mod_w.shape
    tn = n // 4
    return pl.pallas_call(
        _mod_kernel,
        grid=(depth, n // tn),
        in_specs=[pl.BlockSpec(cond.shape, lambda l, j: (0, 0)),
                  pl.BlockSpec((1, d, tn), lambda l, j: (l, 0, j)),
                  pl.BlockSpec((1, 1, tn), lambda l, j: (l, 0, j))],
        out_specs=pl.BlockSpec((1, cond.shape[0], tn), lambda l, j: (l, 0, j)),
        out_shape=jax.ShapeDtypeStruct((depth, cond.shape[0], n), F32),
        compiler_params=_params(),
        name="modulation",
    )(cond, mod_w, mod_b.reshape(depth, 1, n))


def _normmod_kernel(h_ref, g_ref, mod_ref, o_ref, *, rows, ctx_len):
    o_ref[0] = _norm_modulate(h_ref[0], g_ref[...], mod_ref, pl.program_id(1), rows, ctx_len, 0, 1).astype(BF16)


def _mixer_input(h, g, modtab, ctx_len):
    bsz, total, d = h.shape
    rows = _token_tile(total, TOKEN_TILES)
    row_spec = pl.BlockSpec((1, rows, d), lambda b, i: (b, i, 0))
    return pl.pallas_call(
        functools.partial(_normmod_kernel, rows=rows, ctx_len=ctx_len),
        grid=(bsz, total // rows),
        in_specs=[row_spec, _const_spec((1, d)), pl.BlockSpec((1, 2, 6, d), lambda b, i: (b, 0, 0, 0))],
        out_specs=row_spec,
        out_shape=jax.ShapeDtypeStruct((bsz, total, d), BF16),
        compiler_params=_params(),
        name="norm_modulate",
    )(h, g.reshape(1, d), modtab)


def _inproj_kernel(x_ref, w_ref, *rest, with_t):
    if with_t:
        wt_ref, o_ref, ot_ref = rest
    else:
        (o_ref,) = rest
    o_ref[0] = jnp.dot(x_ref[0], w_ref[...], preferred_element_type=F32)
    if with_t:
        ot_ref[0] = lax.dot_general(wt_ref[...], x_ref[0], (((1,), (1,)), ((), ())),
                                    preferred_element_type=F32)


def _in_projection(xm, w, w_t=None):
    bsz, total, d = xm.shape
    n = w.shape[1]
    rows = _token_tile(total, PROJ_TILES)
    in_specs = [pl.BlockSpec((1, rows, d), lambda b, i: (b, i, 0)),
                _weight_spec((d, n))]
    out_specs = [pl.BlockSpec((1, rows, n), lambda b, i: (b, i, 0))]
    out_shape = [jax.ShapeDtypeStruct((bsz, total, n), F32)]
    args = [xm, w]
    if w_t is not None:
        nt = w_t.shape[0]
        in_specs.append(_const_spec((nt, d)))
        out_specs.append(pl.BlockSpec((1, nt, rows), lambda b, i: (b, 0, i)))
        out_shape.append(jax.ShapeDtypeStruct((bsz, nt, total), F32))
        args.append(w_t)
    out = pl.pallas_call(
        functools.partial(_inproj_kernel, with_t=w_t is not None),
        grid=(bsz, total // rows),
        in_specs=in_specs, out_specs=out_specs, out_shape=out_shape,
        compiler_params=_params(),
        name="in_projection",
    )(*args)
    return out if w_t is not None else out[0]


def _retention_kernel(p_ref, cos_ref, sin_ref, lg_ref, *rest, reverse):
    if reverse:
        yf_ref, o_ref, s_ref, dmat_ref, qdec_ref, kdec_ref = rest
    else:
        o_ref, s_ref, dmat_ref, qdec_ref, kdec_ref = rest
    c = RET_CHUNK
    bsz = p_ref.shape[0]
    lg = lg_ref[...]

    @pl.when(pl.program_id(0) == 0)
    def _():
        s_ref[...] = jnp.zeros_like(s_ref)
        pos = _iota2((c, 1), 0).astype(F32)
        qdec_ref[...] = jnp.exp(((c - pos) if reverse else (pos + 1.0)) * lg)
        kdec_ref[...] = jnp.exp((pos if reverse else (c - 1.0 - pos)) * lg)
        ii = _iota2((c, c), 0)
        jj = _iota2((c, c), 1)
        rel = (jj - ii) if reverse else (ii - jj)
        keep = rel >= 0
        relf = jnp.where(keep, rel, 0).astype(F32)
        for h in range(N_HEADS):
            dmat_ref[h] = jnp.where(keep, jnp.exp(relf * lg[:, h * HEAD_DK:(h + 1) * HEAD_DK]), 0.0)

    cos = cos_ref[...]
    sin = sin_ref[...]
    w = N_HEADS * HEAD_DK
    chains = [(b, h) for b in range(bsz) for h in range(N_HEADS)]
    states = [s_ref[b * N_HEADS + h] for b, h in chains]
    qs, ks, vs = [], [], []
    for b, h in chains:
        sl = slice(h * HEAD_DK, (h + 1) * HEAD_DK)
        q = p_ref[b, :, sl]
        k = p_ref[b, :, w + h * HEAD_DK:w + (h + 1) * HEAD_DK] * (HEAD_DK ** -0.5)
        qs.append(q * cos + pltpu.roll(q, HEAD_DK // 2, 1) * sin)
        ks.append(k * cos + pltpu.roll(k, HEAD_DK // 2, 1) * sin)
        vs.append(p_ref[b, :, 2 * w + h * HEAD_DK:2 * w + (h + 1) * HEAD_DK])
    scores = [_mm_nt(q, k) * dmat_ref[h] for q, k, (b, h) in zip(qs, ks, chains)]
    ys = [_mm(jnp.concatenate([sc, q * qdec_ref[:, h * HEAD_DK:(h + 1) * HEAD_DK]], axis=1),
              jnp.concatenate([v, st], axis=0))
          for sc, q, v, st, (b, h) in zip(scores, qs, vs, states, chains)]
    for i, (b, h) in enumerate(chains):
        sl = slice(h * HEAD_DK, (h + 1) * HEAD_DK)
        s_ref[b * N_HEADS + h] = (states[i] * jnp.exp(c * lg[:, sl])
                                  + _mm_tn(ks[i] * kdec_ref[:, sl], vs[i]))
        y = ys[i]
        if reverse:
            y = y + yf_ref[b, :, sl]
            g = p_ref[b, :, 3 * w + h * HEAD_DK:3 * w + (h + 1) * HEAD_DK]
            y = y * lax.rsqrt(jnp.mean(y * y, axis=-1, keepdims=True) + EPS) * _silu(g)
        o_ref[b, :, sl] = y


def _retention(p, cos, sin, log_gamma_rows, ctx_len):
    bsz, total, _ = p.shape
    w = N_HEADS * HEAD_DK
    c = RET_CHUNK
    n_all, n_ctx = total // c, ctx_len // c
    y = None
    for reverse in (False, True):
        def blk(s, reverse=reverse):
            return (0, _scan_block(s, n_ctx, n_all, reverse), 0)

        def tab(s, reverse=reverse):
            return (_scan_block(s, n_ctx, n_all, reverse), 0)

        in_specs = [pl.BlockSpec((bsz, c, 4 * w), blk),
                    pl.BlockSpec((c, HEAD_DK), tab),
                    pl.BlockSpec((c, HEAD_DK), tab),
                    _const_spec((1, w))]
        args = [p, cos, sin, log_gamma_rows[int(reverse)]]
        if reverse:
            in_specs.append(pl.BlockSpec((bsz, c, w), blk))
            args.append(y)
        y = pl.pallas_call(
            functools.partial(_retention_kernel, reverse=reverse),
            grid=(n_all,),
            in_specs=in_specs,
            out_specs=pl.BlockSpec((bsz, c, w), blk),
            out_shape=jax.ShapeDtypeStruct((bsz, total, w), F32),
            scratch_shapes=[pltpu.VMEM((bsz * N_HEADS, HEAD_DK, HEAD_DK), F32),
                            pltpu.VMEM((N_HEADS, c, c), F32),
                            pltpu.VMEM((c, w), F32), pltpu.VMEM((c, w), F32)],
            compiler_params=_params(1),
            name="retention_bwd" if reverse else "retention_fwd",
        )(*args)
    return y


def _lru_kernel(x_ref, xp_ref, xn_ref, cw_ref, cb_ref, gw_ref, gb_ref, lam_ref, *rest,
                reverse, ctx_len, total, n_ctx, n_all):
    if reverse:
        hf_ref, o_ref, buf_ref, carry_ref = rest
    else:
        o_ref, buf_ref, carry_ref = rest
    rows = LRU_BLOCK_ROWS
    s = pl.program_id(1)
    blk = _scan_block(s, n_ctx, n_all, reverse)

    @pl.when(s == 0)
    def _():
        carry_ref[...] = jnp.zeros_like(carry_ref)

    w = x_ref.shape[2] // 2
    _fill_halo(buf_ref, x_ref[0, :, 0:w], xp_ref[0], xn_ref[0], blk * rows, rows, ctx_len, total)
    xc = cb_ref[...] + sum(cw_ref[j:j + 1, :] * _shifted(buf_ref, j - CONV_K // 2, rows) for j in range(CONV_K))
    n_blocks = gw_ref.shape[0]
    bw = w // n_blocks
    parts = [_mm(xc[:, n * bw:(n + 1) * bw], gw_ref[n]) for n in range(n_blocks)]
    r_gate = _sigmoid(jnp.concatenate([p[:, :bw] for p in parts], axis=1) + gb_ref[0:1, :])
    i_gate = _sigmoid(jnp.concatenate([p[:, bw:] for p in parts], axis=1) + gb_ref[1:2, :])
    log_a = LRU_C * r_gate * (-_softplus(-lam_ref[...]))
    a = jnp.exp(log_a)
    b = jnp.sqrt(1.0 - jnp.exp(2.0 * log_a)) * i_gate * xc
    n_groups = rows // HALO
    a = a.reshape(n_groups, HALO, w)
    b = b.reshape(n_groups, HALO, w)
    rin = _iota2((1, HALO, 1), 1)
    d = 1
    while d < HALO:
        if reverse:
            a_s, b_s, ok = pltpu.roll(a, HALO - d, 1), pltpu.roll(b, HALO - d, 1), rin < HALO - d
        else:
            a_s, b_s, ok = pltpu.roll(a, d, 1), pltpu.roll(b, d, 1), rin >= d
        b = b + a * jnp.where(ok, b_s, 0.0)
        a = a * jnp.where(ok, a_s, 1.0)
        d *= 2
    carry = carry_ref[...]
    groups = [None] * n_groups
    for g in (range(n_groups - 1, -1, -1) if reverse else range(n_groups)):
        hg = b[g] + a[g] * carry
        carry = hg[0:1, :] if reverse else hg[HALO - 1:HALO, :]
        groups[g] = hg
    h = jnp.concatenate(groups, axis=0)
    carry_ref[...] = carry
    if reverse:
        h = (h + hf_ref[0]) * _gelu_tanh(x_ref[0, :, w:2 * w])
    o_ref[0] = h


def _rglru(p, conv_w, conv_b, gate_w, gate_b, lam, ctx_len):
    bsz, total, w2 = p.shape
    w = w2 // 2
    rows = LRU_BLOCK_ROWS
    n_all, n_ctx = total // rows, ctx_len // rows
    per = rows // HALO
    n8 = total // HALO
    n_blocks, bw = gate_w.shape[2], gate_w.shape[3]
    gw = jnp.transpose(gate_w, (0, 2, 3, 1, 4)).reshape(N_DIR, n_blocks, bw, 2 * bw).astype(BF16)
    h = None
    for reverse in (False, True):
        def blk(b, s, reverse=reverse):
            return (b, _scan_block(s, n_ctx, n_all, reverse), 0)

        def prev(b, s, reverse=reverse):
            return (b, jnp.maximum(_scan_block(s, n_ctx, n_all, reverse) * per - 1, 0), 0)

        def nxt(b, s, reverse=reverse):
            return (b, jnp.minimum((_scan_block(s, n_ctx, n_all, reverse) + 1) * per, n8 - 1), 0)

        d = int(reverse)
        in_specs = [pl.BlockSpec((1, rows, w2), blk),
                    pl.BlockSpec((1, HALO, w), prev),
                    pl.BlockSpec((1, HALO, w), nxt),
                    _const_spec((CONV_K, w)), _const_spec((1, w)),
                    _const_spec((n_blocks, bw, 2 * bw)), _const_spec((2, w)), _const_spec((1, w))]
        args = [p, p, p, conv_w, conv_b.reshape(1, w), gw[d], gate_b[d], lam[d].reshape(1, w)]
        if reverse:
            in_specs.append(pl.BlockSpec((1, rows, w), blk))
            args.append(h)
        h = pl.pallas_call(
            functools.partial(_lru_kernel, reverse=reverse, ctx_len=ctx_len, total=total,
                              n_ctx=n_ctx, n_all=n_all),
            grid=(bsz, n_all),
            in_specs=in_specs,
            out_specs=pl.BlockSpec((1, rows, w), blk),
            out_shape=jax.ShapeDtypeStruct((bsz, total, w), F32),
            scratch_shapes=[pltpu.VMEM((rows + 2 * HALO, w), F32), pltpu.VMEM((1, w), F32)],
            compiler_params=_params(),
            name="rglru_bwd" if reverse else "rglru_fwd",
        )(*args)
    return h


def _gdn_kernel(x_ref, xp_ref, xn_ref, abt_ref, cw_ref, alog_r_ref, dtb_r_ref, alog_c_ref, dtb_c_ref,
                ng_ref, *rest, reverse, ctx_len, total, n_ctx, n_all):
    if reverse:
        yf_ref, o_ref, buf_ref, s_ref = rest
    else:
        o_ref, buf_ref, s_ref = rest
    c = GDN_CHUNK
    s = pl.program_id(0)
    blk = _scan_block(s, n_ctx, n_all, reverse)
    bsz = x_ref.shape[0]

    @pl.when(s == 0)
    def _():
        s_ref[...] = jnp.zeros_like(s_ref)

    w = N_HEADS * HEAD_DK
    ii = _iota2((c, c), 0)
    jj = _iota2((c, c), 1)
    incl = (ii <= jj) if reverse else (ii >= jj)
    strict = (ii < jj) if reverse else (ii > jj)
    incl_t = (ii >= jj) if reverse else (ii <= jj)
    last = 0 if reverse else c - 1

    chains = [(b, h) for b in range(bsz) for h in range(N_HEADS)]
    states = [s_ref[b * N_HEADS + h] for b, h in chains]
    qs, ks, vbs, kbes, decays, e_gcs, g_lasts, gccs = [], [], [], [], [], [], [], []
    for b in range(bsz):
        _fill_halo(buf_ref.at[b], x_ref[b, :, 0:3 * w], xp_ref[b], xn_ref[b], blk * c, c, ctx_len, total)
        qkv = _silu(sum(cw_ref[j:j + 1, :] * _shifted(buf_ref.at[b], j - CONV_K // 2, c) for j in range(CONV_K)))
        ab = x_ref[b, :, 4 * w:4 * w + LANES]
        abt = abt_ref[b]
        g_col = -jnp.exp(alog_r_ref[...]) * _softplus(ab + dtb_r_ref[...])
        g_row = -jnp.exp(alog_c_ref[...]) * _softplus(abt + dtb_c_ref[...])
        beta_col = _sigmoid(ab)
        gc_col = _tri_mm(jnp.where(incl, 1.0, 0.0), g_col)
        gc_row = _mm_tri(g_row, jnp.where(incl_t, 1.0, 0.0))
        for h in range(N_HEADS):
            col = int(reverse) * N_HEADS + h
            gcc = gc_col[:, col:col + 1]
            gcr = gc_row[col:col + 1, :]
            beta = beta_col[:, 2 * N_HEADS + col:2 * N_HEADS + col + 1]
            decays.append(jnp.where(incl, jnp.exp(jnp.where(incl, gcc - gcr, 0.0)), 0.0))
            q = qkv[:, h * HEAD_DK:(h + 1) * HEAD_DK]
            k = qkv[:, w + h * HEAD_DK:w + (h + 1) * HEAD_DK]
            v = qkv[:, 2 * w + h * HEAD_DK:2 * w + (h + 1) * HEAD_DK]
            qs.append(q * lax.rsqrt(jnp.sum(q * q, axis=-1, keepdims=True) + EPS) * (HEAD_DK ** -0.5))
            k = k * lax.rsqrt(jnp.sum(k * k, axis=-1, keepdims=True) + EPS)
            ks.append(k)
            kbes.append(k * beta)
            vbs.append(v * beta)
            gccs.append(gcc)
            e_gcs.append(jnp.exp(gcc))
            g_lasts.append(gcc[last:last + 1, :])

    kq = [_mm_nt(jnp.concatenate([kb, q], axis=0), k) for kb, q, k in zip(kbes, qs, ks)]
    t_invs = _tri_inverse([jnp.where(strict, m[0:c] * dec, 0.0) for m, dec in zip(kq, decays)], c)
    qks = [m[c:2 * c] * dec for m, dec in zip(kq, decays)]
    uw = [_mm(t, jnp.concatenate([vb, kb * e], axis=1)) for t, vb, kb, e in zip(t_invs, vbs, kbes, e_gcs)]
    ws = [_mm(jnp.concatenate([m[:, HEAD_DK:], q * e], axis=0), st)
          for m, q, e, st in zip(uw, qs, e_gcs, states)]
    v_news = [m[:, :HEAD_DK] - x[0:c] for m, x in zip(uw, ws)]
    ys = [x[c:2 * c] + _mm(qk, vn) for x, qk, vn in zip(ws, qks, v_news)]
    new_states = [st * jnp.exp(gl) + _mm_tn(k * jnp.exp(gl - gcc), vn)
                  for st, gl, k, gcc, vn in zip(states, g_lasts, ks, gccs, v_news)]
    for i, (b, h) in enumerate(chains):
        s_ref[b * N_HEADS + h] = new_states[i]
        y = ys[i]
        if reverse:
            y = y + yf_ref[b, :, h * HEAD_DK:(h + 1) * HEAD_DK]
            z = x_ref[b, :, 3 * w + h * HEAD_DK:3 * w + (h + 1) * HEAD_DK]
            y = y * lax.rsqrt(jnp.mean(y * y, axis=-1, keepdims=True) + EPS) * ng_ref[...] * _silu(z)
        o_ref[b, :, h * HEAD_DK:(h + 1) * HEAD_DK] = y


def _gated_deltanet(p, abt, conv_w, a_log, dt_bias, norm_g, ctx_len):
    bsz, total, pw = p.shape
    w = N_HEADS * HEAD_DK
    c = GDN_CHUNK
    n_all, n_ctx = total // c, ctx_len // c
    per = c // HALO
    n8 = total // HALO
    nt = abt.shape[1]
    flat_alog = a_log.reshape(-1)
    flat_dtb = dt_bias.reshape(-1)
    alog_r = jnp.zeros((1, LANES), F32).at[0, :flat_alog.shape[0]].set(flat_alog)
    dtb_r = jnp.zeros((1, LANES), F32).at[0, :flat_dtb.shape[0]].set(flat_dtb)
    alog_c = jnp.zeros((nt, 1), F32).at[:flat_alog.shape[0], 0].set(flat_alog)
    dtb_c = jnp.zeros((nt, 1), F32).at[:flat_dtb.shape[0], 0].set(flat_dtb)
    y = None
    for reverse in (False, True):
        def blk(s, reverse=reverse):
            return (0, _scan_block(s, n_ctx, n_all, reverse), 0)

        def blk_t(s, reverse=reverse):
            return (0, 0, _scan_block(s, n_ctx, n_all, reverse))

        def prev(s, reverse=reverse):
            return (0, jnp.maximum(_scan_block(s, n_ctx, n_all, reverse) * per - 1, 0), 0)

        def nxt(s, reverse=reverse):
            return (0, jnp.minimum((_scan_block(s, n_ctx, n_all, reverse) + 1) * per, n8 - 1), 0)

        in_specs = [pl.BlockSpec((bsz, c, pw), blk),
                    pl.BlockSpec((bsz, HALO, 3 * w), prev),
                    pl.BlockSpec((bsz, HALO, 3 * w), nxt),
                    pl.BlockSpec((bsz, nt, c), blk_t),
                    _const_spec((CONV_K, 3 * w)),
                    _const_spec((1, LANES)), _const_spec((1, LANES)),
                    _const_spec((nt, 1)), _const_spec((nt, 1)),
                    _const_spec((1, HEAD_DK))]
        args = [p, p, p, abt, conv_w, alog_r, dtb_r, alog_c, dtb_c, norm_g.reshape(1, HEAD_DK)]
        if reverse:
            in_specs.append(pl.BlockSpec((bsz, c, w), blk))
            args.append(y)
        y = pl.pallas_call(
            functools.partial(_gdn_kernel, reverse=reverse, ctx_len=ctx_len, total=total,
                              n_ctx=n_ctx, n_all=n_all),
            grid=(n_all,),
            in_specs=in_specs,
            out_specs=pl.BlockSpec((bsz, c, w), blk),
            out_shape=jax.ShapeDtypeStruct((bsz, total, w), F32),
            scratch_shapes=[pltpu.VMEM((bsz, c + 2 * HALO, 3 * w), F32),
                            pltpu.VMEM((bsz * N_HEADS, HEAD_DK, HEAD_DK), F32)],
            compiler_params=_params(1),
            name="gdn_bwd" if reverse else "gdn_fwd",
        )(*args)
    return y


def _rwkv_kernel(x_ref, xp_ref, xn_ref, mu_ref, w0_ref, w2_ref, a0_ref, a2_ref, kk_ref, ka_ref, hs_ref, *rest,
                 reverse, ctx_len, total, n_ctx, n_all):
    if reverse:
        g2_ref, rk_ref, lng_ref, lnb_ref, yf_ref, o_ref, buf_ref, s_ref = rest
    else:
        o_ref, buf_ref, s_ref = rest
    c = RWKV_CHUNK
    step = pl.program_id(0)
    blk = _scan_block(step, n_ctx, n_all, reverse)
    bsz = x_ref.shape[0]
    n_pairs = s_ref.shape[0] // bsz
    bw = n_pairs * LANES

    @pl.when(step == 0)
    def _():
        s_ref[...] = jnp.zeros_like(s_ref)

    rows = []
    for b in range(bsz):
        xb = x_ref[b]
        _fill_halo(buf_ref.at[b], xb, xp_ref[b], xn_ref[b], blk * c, c, ctx_len, total)
        rows.append(xb + (0.5 * (_shifted(buf_ref.at[b], -1, c) + _shifted(buf_ref.at[b], 1, c)) - xb) * mu_ref[...])
    x = jnp.concatenate(rows, axis=0)
    n = bsz * c
    r = x[:, 0:bw]
    k = x[:, bw:2 * bw]
    v = x[:, 2 * bw:3 * bw]
    off = 3 * bw
    gd = x[:, off:off + RWKV_GATE_RANK]
    off += RWKV_GATE_RANK
    wd = [x[:, off + d * RWKV_RANK:off + (d + 1) * RWKV_RANK] for d in range(N_DIR)]
    off += N_DIR * RWKV_RANK
    ad = [x[:, off + d * RWKV_RANK:off + (d + 1) * RWKV_RANK] for d in range(N_DIR)]

    def per_head_sum(t):
        t1, t2, t3 = _split3(t)
        hs = hs_ref[...]
        return (jnp.dot(t1, hs, preferred_element_type=F32) + jnp.dot(t2, hs, preferred_element_type=F32)
                + jnp.dot(t3, hs, preferred_element_type=F32))

    def a_of(d):
        return _sigmoid(a0_ref[d:d + 1, :] + _mm(ad[d], a2_ref[d]))

    d = int(reverse)
    w_raw = -_softplus(-(w0_ref[d:d + 1, :] + _mm(jnp.tanh(wd[d]), w2_ref[d]))) - 0.5
    logw = -jnp.exp(w_raw)
    a_dir = a_of(d)
    kkv = k * kk_ref[...]
    kkv = kkv * lax.rsqrt(per_head_sum(kkv * kkv) + EPS)
    kd = k * (1.0 + (a_dir - 1.0) * ka_ref[...])

    cshift = c.bit_length() - 1

    def chunk_masks(size):
        ii = _iota2((size, size), 0)
        jj = _iota2((size, size), 1)
        same = (ii >> cshift) == (jj >> cshift)
        return ((ii <= jj) if reverse else (ii >= jj)) & same, ((ii < jj) if reverse else (ii > jj)) & same

    cum = _tri_mm(jnp.where(chunk_masks(n)[0], 1.0, 0.0), logw)
    last = 0 if reverse else c - 1
    cum_tot = jnp.concatenate([jnp.broadcast_to(cum[b * c + last:b * c + last + 1, :], (c, bw))
                               for b in range(bsz)], axis=0)
    w_inv = jnp.exp(-cum)
    w_rest = jnp.exp(cum_tot - cum)
    alpha_t = -kkv * jnp.exp(cum - logw)
    beta = kkv * a_dir
    beta_t = beta * w_inv
    k_t = kd * w_inv
    r_t = r * jnp.exp(cum)
    beta_h = beta * w_rest
    k_h = kd * w_rest

    n2 = 2 * c
    incl, strict = chunk_masks(n2)
    lane = _iota2((1, LANES), 1)
    m0 = jnp.where(lane < RWKV_N, 1.0, 0.0)
    m1 = 1.0 - m0
    incl2 = jnp.concatenate([incl, incl], axis=1)

    def stack(t, b, p):
        t = t[b * c:(b + 1) * c, p * LANES:(p + 1) * LANES]
        return jnp.concatenate([t * m0, t * m1], axis=0)

    chains = [(b, p) for b in range(bsz) for p in range(n_pairs)]
    states = [s_ref[b * n_pairs + p] for b, p in chains]
    ar = [jnp.concatenate([stack(alpha_t, b, p), stack(r_t, b, p)], axis=0) for b, p in chains]
    bk = [jnp.concatenate([stack(beta_t, b, p), stack(k_t, b, p)], axis=0) for b, p in chains]
    vs = [stack(v, b, p) for b, p in chains]
    bkh = [jnp.concatenate([stack(beta_h, b, p), stack(k_h, b, p)], axis=0) for b, p in chains]
    g = [_mm_nt(x1, x2) for x1, x2 in zip(ar, bk)]
    t_invs = _tri_inverse([jnp.where(strict, -m[0:n2, 0:n2], 0.0) for m in g], c)
    mv = [_mm(jnp.where(strict, m[0:n2, n2:2 * n2], 0.0), vv) for m, vv in zip(g, vs)]
    xs = [_mm_nt(x1, st) for x1, st in zip(ar, states)]
    us = [_mm(t, x1[0:n2] + m) for t, x1, m in zip(t_invs, xs, mv)]
    uv = [jnp.concatenate([u, vv], axis=0) for u, vv in zip(us, vs)]
    ys = [x1[n2:2 * n2] + _mm(jnp.where(incl2, m[n2:2 * n2, :], 0.0), w) for x1, m, w in zip(xs, g, uv)]
    for i, (b, p) in enumerate(chains):
        w_tot = jnp.exp(cum_tot[b * c:b * c + 1, p * LANES:(p + 1) * LANES])
        s_ref[b * n_pairs + p] = states[i] * w_tot + _mm_tn(uv[i], bkh[i])
    y = jnp.concatenate([jnp.concatenate([ys[b * n_pairs + p][0:c] + ys[b * n_pairs + p][c:n2]
                                          for p in range(n_pairs)], axis=1) for b in range(bsz)], axis=0)
    if reverse:
        y = y + jnp.concatenate([yf_ref[b] for b in range(bsz)], axis=0)
        mean = per_head_sum(y) * (1.0 / RWKV_N)
        yc = y - mean
        var = per_head_sum(yc * yc) * (1.0 / RWKV_N)
        yn = yc * lax.rsqrt(var + RWKV_LN_EPS) * lng_ref[...] + lnb_ref[...]
        k_sum = kd + k * (1.0 + (a_of(0) - 1.0) * ka_ref[...])
        bonus = per_head_sum(r * k_sum * rk_ref[...]) * v
        y = (yn + bonus) * _mm(_sigmoid(gd), g2_ref[...])
    for b in range(bsz):
        o_ref[b] = y[b * c:(b + 1) * c]


def _rwkv7(p, mu, w0, w2, a0, a2, g2, k_k, k_a, r_k, ln_g, ln_b, ctx_len):
    bsz, total, pw = p.shape
    bw = k_k.shape[0]
    c = RWKV_CHUNK
    n_all, n_ctx = total // c, ctx_len // c
    per = c // HALO
    n8 = total // HALO
    head_id = jnp.arange(bw) // RWKV_N
    head_sum = (head_id[:, None] == head_id[None, :]).astype(BF16)
    y = None
    for reverse in (False, True):
        def blk(s, reverse=reverse):
            return (0, _scan_block(s, n_ctx, n_all, reverse), 0)

        def prev(s, reverse=reverse):
            return (0, jnp.maximum(_scan_block(s, n_ctx, n_all, reverse) * per - 1, 0), 0)

        def nxt(s, reverse=reverse):
            return (0, jnp.minimum((_scan_block(s, n_ctx, n_all, reverse) + 1) * per, n8 - 1), 0)

        in_specs = [pl.BlockSpec((bsz, c, pw), blk),
                    pl.BlockSpec((bsz, HALO, pw), prev),
                    pl.BlockSpec((bsz, HALO, pw), nxt),
                    _const_spec((1, pw)),
                    _const_spec((N_DIR, bw)), _const_spec((N_DIR, RWKV_RANK, bw)),
                    _const_spec((N_DIR, bw)), _const_spec((N_DIR, RWKV_RANK, bw)),
                    _const_spec((1, bw)), _const_spec((1, bw)), _const_spec((bw, bw))]
        args = [p, p, p, mu.reshape(1, pw), w0, w2.astype(BF16), a0, a2.astype(BF16),
                k_k.reshape(1, bw), k_a.reshape(1, bw), head_sum]
        if reverse:
            in_specs += [_const_spec((RWKV_GATE_RANK, bw)), _const_spec((1, bw)), _const_spec((1, bw)),
                         _const_spec((1, bw)), pl.BlockSpec((bsz, c, bw), blk)]
            args += [g2.astype(BF16), r_k.reshape(1, bw), ln_g.reshape(1, bw), ln_b.reshape(1, bw), y]
        y = pl.pallas_call(
            functools.partial(_rwkv_kernel, reverse=reverse, ctx_len=ctx_len, total=total,
                              n_ctx=n_ctx, n_all=n_all),
            grid=(n_all,),
            in_specs=in_specs,
            out_specs=pl.BlockSpec((bsz, c, bw), blk),
            out_shape=jax.ShapeDtypeStruct((bsz, total, bw), F32),
            scratch_shapes=[pltpu.VMEM((bsz, c + 2 * HALO, pw), F32),
                            pltpu.VMEM((bsz * (bw // LANES), LANES, LANES), F32)],
            compiler_params=_params(1),
            name="rwkv_bwd" if reverse else "rwkv_fwd",
        )(*args)
    return y


def _merge_kernel(h_ref, xm_ref, mod_ref, wg_ref, gb_ref, y0_ref, y1_ref, y2_ref, y3_ref, bw_ref, ow_ref,
                  o_ref, *, rows, ctx_len):
    tile = pl.program_id(1)
    h = h_ref[0]
    d = h.shape[1]
    xm = xm_ref[0]
    merged = jnp.zeros_like(h)
    for k, y_ref in enumerate((y0_ref, y1_ref, y2_ref, y3_ref)):
        gate = _sigmoid(jnp.dot(xm, wg_ref[:, k * d:(k + 1) * d], preferred_element_type=F32)
                        + gb_ref[:, k * d:(k + 1) * d])
        merged = merged + gate * _mm(y_ref[0], bw_ref[k])
    o_ref[0] = h + _mod_row(mod_ref, tile, rows, ctx_len, 2) * _mm(merged, ow_ref[...])


def _merge(h, xm, modtab, w_gates, gate_b, ys, branch_w, out_w, ctx_len):
    bsz, total, d = h.shape
    rows = _token_tile(total, TOKEN_TILES)
    bwid = ys[0].shape[2]
    row_spec = lambda width: pl.BlockSpec((1, rows, width), lambda b, i: (b, i, 0))
    return pl.pallas_call(
        functools.partial(_merge_kernel, rows=rows, ctx_len=ctx_len),
        grid=(bsz, total // rows),
        in_specs=[row_spec(d), row_spec(d),
                  pl.BlockSpec((1, 2, 6, d), lambda b, i: (b, 0, 0, 0)),
                  _weight_spec((d, N_BRANCH * d)), _const_spec((1, N_BRANCH * d)),
                  row_spec(bwid), row_spec(bwid), row_spec(bwid), row_spec(bwid),
                  _weight_spec((N_BRANCH, bwid, d)), _weight_spec((d, d))],
        out_specs=row_spec(d),
        out_shape=jax.ShapeDtypeStruct((bsz, total, d), F32),
        compiler_params=_params(),
        name="merge",
    )(h, xm, modtab, w_gates, gate_b.reshape(1, -1), *ys, branch_w, out_w)


def _ffn_kernel(h_ref, g_ref, mod_ref, w1_ref, w3_ref, w2_ref, fg_ref, o_ref, *, rows, ctx_len, skip, final):
    tile = pl.program_id(1) + skip
    h = h_ref[0]
    u = _norm_modulate(h, g_ref[...], mod_ref, tile, rows, ctx_len, 3, 4).astype(BF16)
    dff = w1_ref.shape[1]
    part = dff // FFN_SPLIT
    out = jnp.zeros_like(h)
    for j in range(FFN_SPLIT):
        cols = slice(j * part, (j + 1) * part)
        t = (_silu(jnp.dot(u, w1_ref[:, cols], preferred_element_type=F32))
             * jnp.dot(u, w3_ref[:, cols], preferred_element_type=F32))
        out = out + _mm(t, w2_ref[cols, :])
    h = h + _mod_row(mod_ref, tile, rows, ctx_len, 5) * out
    if final:
        h = h * lax.rsqrt(jnp.mean(h * h, axis=-1, keepdims=True) + EPS) * fg_ref[...]
    o_ref[0] = h


def _ffn(h, g, modtab, w1, w3, w2, final_g, ctx_len, final):
    bsz, total, d = h.shape
    rows = _token_tile(ctx_len, FFN_TILES)
    dff = w1.shape[1]
    skip = ctx_len // rows if final else 0
    out_rows = total - skip * rows
    return pl.pallas_call(
        functools.partial(_ffn_kernel, rows=rows, ctx_len=ctx_len, skip=skip, final=final),
        grid=(bsz, out_rows // rows),
        in_specs=[pl.BlockSpec((1, rows, d), lambda b, i: (b, i + skip, 0)), _const_spec((1, d)),
                  pl.BlockSpec((1, 2, 6, d), lambda b, i: (b, 0, 0, 0)),
                  _weight_spec((d, dff)), _weight_spec((d, dff)), _weight_spec((dff, d)), _const_spec((1, d))],
        out_specs=pl.BlockSpec((1, rows, d), lambda b, i: (b, i, 0)),
        out_shape=jax.ShapeDtypeStruct((bsz, out_rows, d), F32),
        compiler_params=_params(),
        name="ffn",
    )(h, g.reshape(1, d), modtab, w1, w3, w2, final_g.reshape(1, d))


def _rope_tables(seq, ctx_len):
    t = jnp.arange(seq)
    row = (t // GRID_W).astype(F32)
    col = (t % GRID_W).astype(F32)
    quarter = HEAD_DK // 4
    inv = ROPE_BASE ** (-jnp.arange(quarter, dtype=F32) / quarter)
    ang = jnp.concatenate([row[:, None] * inv, col[:, None] * inv], axis=-1)
    cos, sin = jnp.cos(ang), jnp.sin(ang)
    cos = jnp.concatenate([cos, cos], axis=-1)
    sin = jnp.concatenate([-sin, sin], axis=-1)
    cos = jnp.concatenate([jnp.ones((ctx_len, HEAD_DK), F32), cos], axis=0)
    sin = jnp.concatenate([jnp.zeros((ctx_len, HEAD_DK), F32), sin], axis=0)
    return cos, sin


def kernel(x, c, ctx, c_ctx, mod_w, mod_b, norm1_g, norm2_g, in_w, gate_b, ret_decay_exp, lru_conv_w,
           lru_conv_b, lru_gate_w, lru_gate_b, lru_lambda, gdn_conv_w, gdn_a_log, gdn_dt_bias, gdn_norm_g,
           rwkv_mu, rwkv_w0, rwkv_w2, rwkv_a0, rwkv_a2, rwkv_g2, rwkv_k_k, rwkv_k_a, rwkv_r_k, rwkv_ln_g,
           rwkv_ln_b, branch_w, out_w, ffn_w1, ffn_w3, ffn_w2, final_norm_g):
    bsz, seq, d = x.shape
    ctx_len = ctx.shape[1]
    depth = mod_w.shape[0]
    bw = d // 2
    assert ctx_len % LRU_BLOCK_ROWS == 0 and seq % LRU_BLOCK_ROWS == 0 and bsz + 1 <= HALO

    h = jnp.concatenate([ctx, x], axis=1)
    cond = jnp.zeros((HALO, d), F32).at[:bsz].set(c).at[bsz].set(c_ctx)
    mods = _modulation(cond, mod_w, mod_b).reshape(depth, HALO, 6, d)
    cos, sin = _rope_tables(seq, ctx_len)

    widths = (4 * bw, 2 * bw, 3 * bw, bw, N_DIR * N_HEADS, N_DIR * N_HEADS,
              3 * bw + RWKV_GATE_RANK + 2 * N_DIR * RWKV_RANK, N_BRANCH * d)
    offs = [0]
    for wdt in widths:
        offs.append(offs[-1] + wdt)

    for l in range(depth):
        lat = mods[l, :bsz]
        modtab = jnp.stack([jnp.broadcast_to(mods[l, bsz], lat.shape), lat], axis=1)
        wl = in_w[l]
        w_ret = wl[:, offs[0]:offs[1]].astype(BF16)
        w_lru = wl[:, offs[1]:offs[2]].astype(BF16)
        w_ab = wl[:, offs[4]:offs[6]]
        w_gdn = jnp.concatenate([wl[:, offs[2]:offs[4]], w_ab,
                                 jnp.zeros((d, LANES - w_ab.shape[1]), F32)], axis=1).astype(BF16)
        w_abt = w_ab.T.astype(BF16)
        w_rwkv = wl[:, offs[6]:offs[7]].astype(BF16)
        w_gates = wl[:, offs[7]:offs[8]].astype(BF16)

        xm = _mixer_input(h, norm1_g[l], modtab, ctx_len)
        p_ret = _in_projection(xm, w_ret)
        p_lru = _in_projection(xm, w_lru)
        p_gdn, abt = _in_projection(xm, w_gdn, w_t=w_abt)
        p_rwkv = _in_projection(xm, w_rwkv)

        log_gamma = jnp.log1p(-jnp.exp2(-ret_decay_exp[l].astype(F32)))
        lg_rows = jnp.repeat(log_gamma, HEAD_DK, axis=1).reshape(N_DIR, 1, N_HEADS * HEAD_DK)
        y_ret = _retention(p_ret, cos, sin, lg_rows, ctx_len)
        y_lru = _rglru(p_lru, lru_conv_w[l], lru_conv_b[l], lru_gate_w[l], lru_gate_b[l], lru_lambda[l], ctx_len)
        y_gdn = _gated_deltanet(p_gdn, abt, gdn_conv_w[l], gdn_a_log[l], gdn_dt_bias[l], gdn_norm_g[l], ctx_len)
        y_rwkv = _rwkv7(p_rwkv, rwkv_mu[l], rwkv_w0[l], rwkv_w2[l], rwkv_a0[l], rwkv_a2[l], rwkv_g2[l],
                        rwkv_k_k[l], rwkv_k_a[l], rwkv_r_k[l].reshape(-1), rwkv_ln_g[l], rwkv_ln_b[l], ctx_len)

        h = _merge(h, xm, modtab, w_gates, gate_b[l], (y_ret, y_lru, y_gdn, y_rwkv),
                   branch_w[l].astype(BF16), out_w[l].astype(BF16), ctx_len)
        h = _ffn(h, norm2_g[l], modtab, ffn_w1[l].astype(BF16), ffn_w3[l].astype(BF16),
                 ffn_w2[l].astype(BF16), final_norm_g, ctx_len, final=(l == depth - 1))
    return h
```

```python
import functools

import jax
import jax.numpy as jnp
from jax import lax
from jax.experimental import pallas as pl
from jax.experimental.pallas import tpu as pltpu

F32 = jnp.float32
BF16 = jnp.bfloat16

GRID_W = 64
ROPE_BASE = 10000.0
EPS = 1e-6
RWKV_LN_EPS = 64e-5
LRU_C = 8.0
N_DIR = 2
N_BRANCH = 4
HEAD_DK = 128
N_HEADS = 4
RWKV_N = 64
RWKV_RANK = 64
RWKV_GATE_RANK = 128
CONV_K = 4
LANES = 128
HALO = 8
VMEM_LIMIT_BYTES = 56 * 1024 * 1024

TOKEN_TILES = (768, 512, 256)
PROJ_TILES = (1408,) + TOKEN_TILES
RET_CHUNK = 128
LRU_BLOCK_ROWS = 256
GDN_CHUNK = 128
RWKV_CHUNK = 64
FFN_TILES = (256,)
FFN_SPLIT = 1


def _mm(a, b):
    return jnp.dot(a.astype(BF16), b.astype(BF16), preferred_element_type=F32)


def _mm_nt(a, b):
    return lax.dot_general(a.astype(BF16), b.astype(BF16), (((1,), (1,)), ((), ())),
                           preferred_element_type=F32)


def _mm_tn(a, b):
    return lax.dot_general(a.astype(BF16), b.astype(BF16), (((0,), (0,)), ((), ())),
                           preferred_element_type=F32)


def _split3(x):
    x1 = x.astype(BF16)
    r1 = x - x1.astype(F32)
    x2 = r1.astype(BF16)
    x3 = (r1 - x2.astype(F32)).astype(BF16)
    return x1, x2, x3


def _tri_mm(tri, x):
    x1, x2, x3 = _split3(x)
    t = tri.astype(BF16)
    return (jnp.dot(t, x1, preferred_element_type=F32) + jnp.dot(t, x2, preferred_element_type=F32)
            + jnp.dot(t, x3, preferred_element_type=F32))


def _mm_tri(x, tri):
    x1, x2, x3 = _split3(x)
    t = tri.astype(BF16)
    return (jnp.dot(x1, t, preferred_element_type=F32) + jnp.dot(x2, t, preferred_element_type=F32)
            + jnp.dot(x3, t, preferred_element_type=F32))


def _sigmoid(x):
    return 1.0 / (1.0 + jnp.exp(-x))


def _silu(x):
    return x * _sigmoid(x)


def _softplus(x):
    return jnp.maximum(x, 0.0) + jnp.log1p(jnp.exp(-jnp.abs(x)))


def _gelu_tanh(x):
    return 0.5 * x * (1.0 + jnp.tanh(0.7978845608028654 * (x + 0.044715 * x * x * x)))


def _iota2(shape, axis):
    return lax.broadcasted_iota(jnp.int32, shape, axis)


def _scan_block(s, n_ctx, n_all, reverse):
    if not reverse:
        return s
    return jnp.where(s < n_ctx, n_ctx - 1 - s, n_all - 1 - (s - n_ctx))


def _tri_inverse(mats, n_max):
    n = mats[0].shape[0]
    ii = _iota2((n, n), 0)
    jj = _iota2((n, n), 1)
    eye = jnp.where(ii == jj, 1.0, 0.0)
    pair = (ii >> 1) == (jj >> 1)
    ts = [eye - jnp.where(pair, a, 0.0) for a in mats]
    b = 2
    while b < n_max:
        sh = b.bit_length() - 1
        between = ((ii >> (sh + 1)) == (jj >> (sh + 1))) & ((ii >> sh) != (jj >> sh))
        xs = [_mm(t, jnp.where(between, a, 0.0)) for t, a in zip(ts, mats)]
        ts = [t - _mm(x, t) for t, x in zip(ts, xs)]
        b *= 2
    return ts


def _fill_halo(buf_ref, x, prev, nxt, row0, rows, ctx_len, total):
    prev_ok = jnp.logical_and(row0 != 0, row0 != ctx_len)
    next_ok = jnp.logical_and(row0 + rows != ctx_len, row0 + rows != total)
    buf_ref[0:HALO, :] = jnp.where(prev_ok, prev, 0.0)
    buf_ref[HALO:HALO + rows, :] = x
    buf_ref[HALO + rows:HALO + rows + HALO, :] = jnp.where(next_ok, nxt, 0.0)


def _shifted(buf_ref, off, rows):
    return buf_ref[HALO + off:HALO + off + rows, :]


def _shift_conv(x, prev, nxt, taps_ref, row0, ctx_len, total):
    rows = x.shape[0]
    centre = CONV_K // 2
    offs = [j - centre for j in range(CONV_K) if j != centre]
    prev = jnp.where(jnp.logical_and(row0 != 0, row0 != ctx_len), prev, 0.0)
    nxt = jnp.where(jnp.logical_and(row0 + rows != ctx_len, row0 + rows != total), nxt, 0.0)
    ii = _iota2((rows, 2 * rows), 0)
    jj = _iota2((rows, 2 * rows), 1) & (rows - 1)
    shifts = jnp.concatenate([jnp.where(jj == ii + o, 1.0, 0.0) for o in offs], axis=0).astype(BF16)
    hi = x.astype(BF16)
    lo = (x - hi.astype(F32)).astype(BF16)
    moved = jnp.dot(shifts, jnp.concatenate([hi, lo], axis=0), preferred_element_type=F32)
    out = taps_ref[centre:centre + 1, :] * x
    r8 = _iota2((HALO, 1), 0)
    head = jnp.zeros_like(prev)
    tail = jnp.zeros_like(nxt)
    for n, o in enumerate(offs):
        tap = taps_ref[o + centre:o + centre + 1, :]
        out = out + tap * moved[n * rows:(n + 1) * rows]
        if o < 0:
            head = head + tap * jnp.where(r8 < -o, pltpu.roll(prev, -o, 0), 0.0)
        else:
            tail = tail + tap * jnp.where(r8 >= HALO - o, pltpu.roll(nxt, HALO - o, 0), 0.0)
    return jnp.concatenate([out[:HALO] + head, out[HALO:rows - HALO], out[rows - HALO:] + tail], axis=0)


def _norm_modulate(x, g, mod_ref, tile, rows, ctx_len, shift_idx, scale_idx):
    xn = x * lax.rsqrt(jnp.mean(x * x, axis=-1, keepdims=True) + EPS) * g
    is_ctx = (tile * rows + _iota2((rows, 1), 0)) < ctx_len
    shift = jnp.where(is_ctx, mod_ref[0, 0, shift_idx:shift_idx + 1, :], mod_ref[0, 1, shift_idx:shift_idx + 1, :])
    scale = jnp.where(is_ctx, mod_ref[0, 0, scale_idx:scale_idx + 1, :], mod_ref[0, 1, scale_idx:scale_idx + 1, :])
    return xn * (1.0 + scale) + shift


def _mod_row(mod_ref, tile, rows, ctx_len, idx):
    is_ctx = (tile * rows + _iota2((rows, 1), 0)) < ctx_len
    return jnp.where(is_ctx, mod_ref[0, 0, idx:idx + 1, :], mod_ref[0, 1, idx:idx + 1, :])


def _params(n_axes=2):
    return pltpu.CompilerParams(dimension_semantics=("arbitrary",) * n_axes,
                                vmem_limit_bytes=VMEM_LIMIT_BYTES)


def _const_spec(shape):
    nd = len(shape)
    return pl.BlockSpec(shape, lambda *_: (0,) * nd)


def _weight_spec(shape):
    nd = len(shape)
    return pl.BlockSpec(shape, lambda *_: (0,) * nd, pipeline_mode=pl.Buffered(1))


def _token_tile(total, candidates):
    for rows in candidates:
        if total % rows == 0:
            return rows
    raise ValueError(f"token count {total} is not a multiple of {candidates[-1]}")


def _mod_kernel(c_ref, w_ref, b_ref, o_ref):
    o_ref[0] = _mm(_silu(c_ref[...]), w_ref[0]) + b_ref[0]


def _modulation(cond, mod_w, mod_b):
    depth, d, n = mod_w.shape
    tn = n // 4
    return pl.pallas_call(
        _mod_kernel,
        grid=(depth, n // tn),
        in_specs=[pl.BlockSpec(cond.shape, lambda l, j: (0, 0)),
                  pl.BlockSpec((1, d, tn), lambda l, j: (l, 0, j)),
                  pl.BlockSpec((1, 1, tn), lambda l, j: (l, 0, j))],
        out_specs=pl.BlockSpec((1, cond.shape[0], tn), lambda l, j: (l, 0, j)),
        out_shape=jax.ShapeDtypeStruct((depth, cond.shape[0], n), F32),
        compiler_params=_params(),
        name="modulation",
    )(cond, mod_w, mod_b.reshape(depth, 1, n))


def _normmod_kernel(h_ref, g_ref, mod_ref, o_ref, *, rows, ctx_len):
    o_ref[0] = _norm_modulate(h_ref[0], g_ref[...], mod_ref, pl.program_id(1), rows, ctx_len, 0, 1).astype(BF16)


def _mixer_input(h, g, modtab, ctx_len):
    bsz, total, d = h.shape
    rows = _token_tile(total, TOKEN_TILES)
    row_spec = pl.BlockSpec((1, rows, d), lambda b, i: (b, i, 0))
    return pl.pallas_call(
        functools.partial(_normmod_kernel, rows=rows, ctx_len=ctx_len),
        grid=(bsz, total // rows),
        in_specs=[row_spec, _const_spec((1, d)), pl.BlockSpec((1, 2, 6, d), lambda b, i: (b, 0, 0, 0))],
        out_specs=row_spec,
        out_shape=jax.ShapeDtypeStruct((bsz, total, d), BF16),
        compiler_params=_params(),
        name="norm_modulate",
    )(h, g.reshape(1, d), modtab)


def _inproj_kernel(x_ref, w_ref, *rest, with_t):
    if with_t:
        wt_ref, o_ref, ot_ref = rest
    else:
        (o_ref,) = rest
    o_ref[0] = jnp.dot(x_ref[0], w_ref[...], preferred_element_type=F32)
    if with_t:
        ot_ref[0] = lax.dot_general(wt_ref[...], x_ref[0], (((1,), (1,)), ((), ())),
                                    preferred_element_type=F32)


def _in_projection(xm, w, w_t=None):
    bsz, total, d = xm.shape
    n = w.shape[1]
    rows = _token_tile(total, PROJ_TILES)
    in_specs = [pl.BlockSpec((1, rows, d), lambda b, i: (b, i, 0)),
                _weight_spec((d, n))]
    out_specs = [pl.BlockSpec((1, rows, n), lambda b, i: (b, i, 0))]
    out_shape = [jax.ShapeDtypeStruct((bsz, total, n), F32)]
    args = [xm, w]
    if w_t is not None:
        nt = w_t.shape[0]
        in_specs.append(_const_spec((nt, d)))
        out_specs.append(pl.BlockSpec((1, nt, rows), lambda b, i: (b, 0, i)))
        out_shape.append(jax.ShapeDtypeStruct((bsz, nt, total), F32))
        args.append(w_t)
    out = pl.pallas_call(
        functools.partial(_inproj_kernel, with_t=w_t is not None),
        grid=(bsz, total // rows),
        in_specs=in_specs, out_specs=out_specs, out_shape=out_shape,
        compiler_params=_params(),
        name="in_projection",
    )(*args)
    return out if w_t is not None else out[0]


def _retention_kernel(p_ref, cos_ref, sin_ref, lg_ref, *rest, reverse):
    if reverse:
        yf_ref, o_ref, s_ref, dmat_ref, qdec_ref, kdec_ref = rest
    else:
        o_ref, s_ref, dmat_ref, qdec_ref, kdec_ref = rest
    c = RET_CHUNK
    bsz = p_ref.shape[0]
    lg = lg_ref[...]

    @pl.when(pl.program_id(0) == 0)
    def _():
        s_ref[...] = jnp.zeros_like(s_ref)
        pos = _iota2((c, 1), 0).astype(F32)
        qdec_ref[...] = jnp.exp(((c - pos) if reverse else (pos + 1.0)) * lg)
        kdec_ref[...] = jnp.exp((pos if reverse else (c - 1.0 - pos)) * lg)
        ii = _iota2((c, c), 0)
        jj = _iota2((c, c), 1)
        rel = (jj - ii) if reverse else (ii - jj)
        keep = rel >= 0
        relf = jnp.where(keep, rel, 0).astype(F32)
        for h in range(N_HEADS):
            dmat_ref[h] = jnp.where(keep, jnp.exp(relf * lg[:, h * HEAD_DK:(h + 1) * HEAD_DK]), 0.0)

    cos = cos_ref[...]
    sin = sin_ref[...]
    w = N_HEADS * HEAD_DK
    chains = [(b, h) for b in range(bsz) for h in range(N_HEADS)]
    states = [s_ref[b * N_HEADS + h] for b, h in chains]
    qs, ks, vs = [], [], []
    for b, h in chains:
        sl = slice(h * HEAD_DK, (h + 1) * HEAD_DK)
        q = p_ref[b, :, sl]
        k = p_ref[b, :, w + h * HEAD_DK:w + (h + 1) * HEAD_DK] * (HEAD_DK ** -0.5)
        qs.append(q * cos + pltpu.roll(q, HEAD_DK // 2, 1) * sin)
        ks.append(k * cos + pltpu.roll(k, HEAD_DK // 2, 1) * sin)
        vs.append(p_ref[b, :, 2 * w + h * HEAD_DK:2 * w + (h + 1) * HEAD_DK])
    scores = [_mm_nt(q, k) * dmat_ref[h] for q, k, (b, h) in zip(qs, ks, chains)]
    ys = [_mm(jnp.concatenate([sc, q * qdec_ref[:, h * HEAD_DK:(h + 1) * HEAD_DK]], axis=1),
              jnp.concatenate([v, st], axis=0))
          for sc, q, v, st, (b, h) in zip(scores, qs, vs, states, chains)]
    for i, (b, h) in enumerate(chains):
        sl = slice(h * HEAD_DK, (h + 1) * HEAD_DK)
        s_ref[b * N_HEADS + h] = (states[i] * jnp.exp(c * lg[:, sl])
                                  + _mm_tn(ks[i] * kdec_ref[:, sl], vs[i]))
        y = ys[i]
        if reverse:
            y = y + yf_ref[b, :, sl]
            g = p_ref[b, :, 3 * w + h * HEAD_DK:3 * w + (h + 1) * HEAD_DK]
            y = y * lax.rsqrt(jnp.mean(y * y, axis=-1, keepdims=True) + EPS) * _silu(g)
        o_ref[b, :, sl] = y


def _retention(p, cos, sin, log_gamma_rows, ctx_len):
    bsz, total, _ = p.shape
    w = N_HEADS * HEAD_DK
    c = RET_CHUNK
    n_all, n_ctx = total // c, ctx_len // c
    y = None
    for reverse in (False, True):
        def blk(s, reverse=reverse):
            return (0, _scan_block(s, n_ctx, n_all, reverse), 0)

        def tab(s, reverse=reverse):
            return (_scan_block(s, n_ctx, n_all, reverse), 0)

        in_specs = [pl.BlockSpec((bsz, c, 4 * w), blk),
                    pl.BlockSpec((c, HEAD_DK), tab),
                    pl.BlockSpec((c, HEAD_DK), tab),
                    _const_spec((1, w))]
        args = [p, cos, sin, log_gamma_rows[int(reverse)]]
        if reverse:
            in_specs.append(pl.BlockSpec((bsz, c, w), blk))
            args.append(y)
        y = pl.pallas_call(
            functools.partial(_retention_kernel, reverse=reverse),
            grid=(n_all,),
            in_specs=in_specs,
            out_specs=pl.BlockSpec((bsz, c, w), blk),
            out_shape=jax.ShapeDtypeStruct((bsz, total, w), F32),
            scratch_shapes=[pltpu.VMEM((bsz * N_HEADS, HEAD_DK, HEAD_DK), F32),
                            pltpu.VMEM((N_HEADS, c, c), F32),
                            pltpu.VMEM((c, w), F32), pltpu.VMEM((c, w), F32)],
            compiler_params=_params(1),
            name="retention_bwd" if reverse else "retention_fwd",
        )(*args)
    return y


def _lru_kernel(x_ref, xp_ref, xn_ref, cw_ref, cb_ref, gw_ref, gb_ref, lam_ref, *rest,
                reverse, ctx_len, total, n_ctx, n_all):
    if reverse:
        hf_ref, o_ref, buf_ref, carry_ref = rest
    else:
        o_ref, buf_ref, carry_ref = rest
    rows = LRU_BLOCK_ROWS
    s = pl.program_id(1)
    blk = _scan_block(s, n_ctx, n_all, reverse)

    @pl.when(s == 0)
    def _():
        carry_ref[...] = jnp.zeros_like(carry_ref)

    w = x_ref.shape[2] // 2
    _fill_halo(buf_ref, x_ref[0, :, 0:w], xp_ref[0], xn_ref[0], blk * rows, rows, ctx_len, total)
    xc = cb_ref[...] + sum(cw_ref[j:j + 1, :] * _shifted(buf_ref, j - CONV_K // 2, rows) for j in range(CONV_K))
    n_blocks = gw_ref.shape[0]
    bw = w // n_blocks
    parts = [_mm(xc[:, n * bw:(n + 1) * bw], gw_ref[n]) for n in range(n_blocks)]
    r_gate = _sigmoid(jnp.concatenate([p[:, :bw] for p in parts], axis=1) + gb_ref[0:1, :])
    i_gate = _sigmoid(jnp.concatenate([p[:, bw:] for p in parts], axis=1) + gb_ref[1:2, :])
    log_a = LRU_C * r_gate * (-_softplus(-lam_ref[...]))
    a = jnp.exp(log_a)
    b = jnp.sqrt(1.0 - jnp.exp(2.0 * log_a)) * i_gate * xc
    n_groups = rows // HALO
    a = a.reshape(n_groups, HALO, w)
    b = b.reshape(n_groups, HALO, w)
    rin = _iota2((1, HALO, 1), 1)
    d = 1
    while d < HALO:
        if reverse:
            a_s, b_s, ok = pltpu.roll(a, HALO - d, 1), pltpu.roll(b, HALO - d, 1), rin < HALO - d
        else:
            a_s, b_s, ok = pltpu.roll(a, d, 1), pltpu.roll(b, d, 1), rin >= d
        b = b + a * jnp.where(ok, b_s, 0.0)
        a = a * jnp.where(ok, a_s, 1.0)
        d *= 2
    carry = carry_ref[...]
    groups = [None] * n_groups
    for g in (range(n_groups - 1, -1, -1) if reverse else range(n_groups)):
        hg = b[g] + a[g] * carry
        carry = hg[0:1, :] if reverse else hg[HALO - 1:HALO, :]
        groups[g] = hg
    h = jnp.concatenate(groups, axis=0)
    carry_ref[...] = carry
    if reverse:
        h = (h + hf_ref[0]) * _gelu_tanh(x_ref[0, :, w:2 * w])
    o_ref[0] = h


def _rglru(p, conv_w, conv_b, gate_w, gate_b, lam, ctx_len):
    bsz, total, w2 = p.shape
    w = w2 // 2
    rows = LRU_BLOCK_ROWS
    n_all, n_ctx = total // rows, ctx_len // rows
    per = rows // HALO
    n8 = total // HALO
    n_blocks, bw = gate_w.shape[2], gate_w.shape[3]
    gw = jnp.transpose(gate_w, (0, 2, 3, 1, 4)).reshape(N_DIR, n_blocks, bw, 2 * bw).astype(BF16)
    h = None
    for reverse in (False, True):
        def blk(b, s, reverse=reverse):
            return (b, _scan_block(s, n_ctx, n_all, reverse), 0)

        def prev(b, s, reverse=reverse):
            return (b, jnp.maximum(_scan_block(s, n_ctx, n_all, reverse) * per - 1, 0), 0)

        def nxt(b, s, reverse=reverse):
            return (b, jnp.minimum((_scan_block(s, n_ctx, n_all, reverse) + 1) * per, n8 - 1), 0)

        d = int(reverse)
        in_specs = [pl.BlockSpec((1, rows, w2), blk),
                    pl.BlockSpec((1, HALO, w), prev),
                    pl.BlockSpec((1, HALO, w), nxt),
                    _const_spec((CONV_K, w)), _const_spec((1, w)),
                    _const_spec((n_blocks, bw, 2 * bw)), _const_spec((2, w)), _const_spec((1, w))]
        args = [p, p, p, conv_w, conv_b.reshape(1, w), gw[d], gate_b[d], lam[d].reshape(1, w)]
        if reverse:
            in_specs.append(pl.BlockSpec((1, rows, w), blk))
            args.append(h)
        h = pl.pallas_call(
            functools.partial(_lru_kernel, reverse=reverse, ctx_len=ctx_len, total=total,
                              n_ctx=n_ctx, n_all=n_all),
            grid=(bsz, n_all),
            in_specs=in_specs,
            out_specs=pl.BlockSpec((1, rows, w), blk),
            out_shape=jax.ShapeDtypeStruct((bsz, total, w), F32),
            scratch_shapes=[pltpu.VMEM((rows + 2 * HALO, w), F32), pltpu.VMEM((1, w), F32)],
            compiler_params=_params(),
            name="rglru_bwd" if reverse else "rglru_fwd",
        )(*args)
    return h


def _gdn_kernel(x_ref, xp_ref, xn_ref, abt_ref, cw_ref, alog_r_ref, dtb_r_ref, alog_c_ref, dtb_c_ref,
                ng_ref, *rest, reverse, ctx_len, total, n_ctx, n_all):
    if reverse:
        yf_ref, o_ref, s_ref = rest
    else:
        o_ref, s_ref = rest
    c = GDN_CHUNK
    s = pl.program_id(0)
    blk = _scan_block(s, n_ctx, n_all, reverse)
    bsz = x_ref.shape[0]

    @pl.when(s == 0)
    def _():
        s_ref[...] = jnp.zeros_like(s_ref)

    w = N_HEADS * HEAD_DK
    ii = _iota2((c, c), 0)
    jj = _iota2((c, c), 1)
    incl = (ii <= jj) if reverse else (ii >= jj)
    strict = (ii < jj) if reverse else (ii > jj)
    incl_t = (ii >= jj) if reverse else (ii <= jj)
    last = 0 if reverse else c - 1

    chains = [(b, h) for b in range(bsz) for h in range(N_HEADS)]
    states = [s_ref[b * N_HEADS + h] for b, h in chains]
    qs, ks, vbs, kbes, decays, e_gcs, g_lasts, gccs = [], [], [], [], [], [], [], []
    for b in range(bsz):
        qkv = _silu(_shift_conv(x_ref[b, :, 0:3 * w], xp_ref[b], xn_ref[b], cw_ref, blk * c, ctx_len, total))
        ab = x_ref[b, :, 4 * w:4 * w + LANES]
        abt = abt_ref[b]
        g_col = -jnp.exp(alog_r_ref[...]) * _softplus(ab + dtb_r_ref[...])
        g_row = -jnp.exp(alog_c_ref[...]) * _softplus(abt + dtb_c_ref[...])
        beta_col = _sigmoid(ab)
        gc_col = _tri_mm(jnp.where(incl, 1.0, 0.0), g_col)
        gc_row = _mm_tri(g_row, jnp.where(incl_t, 1.0, 0.0))
        for h in range(N_HEADS):
            col = int(reverse) * N_HEADS + h
            gcc = gc_col[:, col:col + 1]
            gcr = gc_row[col:col + 1, :]
            beta = beta_col[:, 2 * N_HEADS + col:2 * N_HEADS + col + 1]
            decays.append(jnp.where(incl, jnp.exp(jnp.where(incl, gcc - gcr, 0.0)), 0.0))
            q = qkv[:, h * HEAD_DK:(h + 1) * HEAD_DK]
            k = qkv[:, w + h * HEAD_DK:w + (h + 1) * HEAD_DK]
            v = qkv[:, 2 * w + h * HEAD_DK:2 * w + (h + 1) * HEAD_DK]
            qs.append(q * lax.rsqrt(jnp.sum(q * q, axis=-1, keepdims=True) + EPS) * (HEAD_DK ** -0.5))
            k = k * lax.rsqrt(jnp.sum(k * k, axis=-1, keepdims=True) + EPS)
            ks.append(k)
            kbes.append(k * beta)
            vbs.append(v * beta)
            gccs.append(gcc)
            e_gcs.append(jnp.exp(gcc))
            g_lasts.append(gcc[last:last + 1, :])

    kq = [_mm_nt(jnp.concatenate([kb, q], axis=0), k) for kb, q, k in zip(kbes, qs, ks)]
    t_invs = _tri_inverse([jnp.where(strict, m[0:c] * dec, 0.0) for m, dec in zip(kq, decays)], c)
    qks = [m[c:2 * c] * dec for m, dec in zip(kq, decays)]
    uw = [_mm(t, jnp.concatenate([vb, kb * e], axis=1)) for t, vb, kb, e in zip(t_invs, vbs, kbes, e_gcs)]
    ws = [_mm(jnp.concatenate([m[:, HEAD_DK:], q * e], axis=0), st)
          for m, q, e, st in zip(uw, qs, e_gcs, states)]
    v_news = [m[:, :HEAD_DK] - x[0:c] for m, x in zip(uw, ws)]
    ys = [x[c:2 * c] + _mm(qk, vn) for x, qk, vn in zip(ws, qks, v_news)]
    new_states = [st * jnp.exp(gl) + _mm_tn(k * jnp.exp(gl - gcc), vn)
                  for st, gl, k, gcc, vn in zip(states, g_lasts, ks, gccs, v_news)]
    for i, (b, h) in enumerate(chains):
        s_ref[b * N_HEADS + h] = new_states[i]
        y = ys[i]
        if reverse:
            y = y + yf_ref[b, :, h * HEAD_DK:(h + 1) * HEAD_DK]
            z = x_ref[b, :, 3 * w + h * HEAD_DK:3 * w + (h + 1) * HEAD_DK]
            y = y * lax.rsqrt(jnp.mean(y * y, axis=-1, keepdims=True) + EPS) * ng_ref[...] * _silu(z)
        o_ref[b, :, h * HEAD_DK:(h + 1) * HEAD_DK] = y


def _gated_deltanet(p, abt, conv_w, a_log, dt_bias, norm_g, ctx_len):
    bsz, total, pw = p.shape
    w = N_HEADS * HEAD_DK
    c = GDN_CHUNK
    n_all, n_ctx = total // c, ctx_len // c
    per = c // HALO
    n8 = total // HALO
    nt = abt.shape[1]
    flat_alog = a_log.reshape(-1)
    flat_dtb = dt_bias.reshape(-1)
    alog_r = jnp.zeros((1, LANES), F32).at[0, :flat_alog.shape[0]].set(flat_alog)
    dtb_r = jnp.zeros((1, LANES), F32).at[0, :flat_dtb.shape[0]].set(flat_dtb)
    alog_c = jnp.zeros((nt, 1), F32).at[:flat_alog.shape[0], 0].set(flat_alog)
    dtb_c = jnp.zeros((nt, 1), F32).at[:flat_dtb.shape[0], 0].set(flat_dtb)
    y = None
    for reverse in (False, True):
        def blk(s, reverse=reverse):
            return (0, _scan_block(s, n_ctx, n_all, reverse), 0)

        def blk_t(s, reverse=reverse):
            return (0, 0, _scan_block(s, n_ctx, n_all, reverse))

        def prev(s, reverse=reverse):
            return (0, jnp.maximum(_scan_block(s, n_ctx, n_all, reverse) * per - 1, 0), 0)

        def nxt(s, reverse=reverse):
            return (0, jnp.minimum((_scan_block(s, n_ctx, n_all, reverse) + 1) * per, n8 - 1), 0)

        in_specs = [pl.BlockSpec((bsz, c, pw), blk),
                    pl.BlockSpec((bsz, HALO, 3 * w), prev),
                    pl.BlockSpec((bsz, HALO, 3 * w), nxt),
                    pl.BlockSpec((bsz, nt, c), blk_t),
                    _const_spec((CONV_K, 3 * w)),
                    _const_spec((1, LANES)), _const_spec((1, LANES)),
                    _const_spec((nt, 1)), _const_spec((nt, 1)),
                    _const_spec((1, HEAD_DK))]
        args = [p, p, p, abt, conv_w, alog_r, dtb_r, alog_c, dtb_c, norm_g.reshape(1, HEAD_DK)]
        if reverse:
            in_specs.append(pl.BlockSpec((bsz, c, w), blk))
            args.append(y)
        y = pl.pallas_call(
            functools.partial(_gdn_kernel, reverse=reverse, ctx_len=ctx_len, total=total,
                              n_ctx=n_ctx, n_all=n_all),
            grid=(n_all,),
            in_specs=in_specs,
            out_specs=pl.BlockSpec((bsz, c, w), blk),
            out_shape=jax.ShapeDtypeStruct((bsz, total, w), F32),
            scratch_shapes=[pltpu.VMEM((bsz * N_HEADS, HEAD_DK, HEAD_DK), F32)],
            compiler_params=_params(1),
            name="gdn_bwd" if reverse else "gdn_fwd",
        )(*args)
    return y


def _rwkv_kernel(x_ref, xp_ref, xn_ref, mu_ref, w0_ref, w2_ref, a0_ref, a2_ref, kk_ref, ka_ref, hs_ref, *rest,
                 reverse, ctx_len, total, n_ctx, n_all):
    if reverse:
        g2_ref, rk_ref, lng_ref, lnb_ref, yf_ref, o_ref, buf_ref, s_ref = rest
    else:
        o_ref, buf_ref, s_ref = rest
    c = RWKV_CHUNK
    step = pl.program_id(0)
    blk = _scan_block(step, n_ctx, n_all, reverse)
    bsz = x_ref.shape[0]
    n_pairs = s_ref.shape[0] // bsz
    bw = n_pairs * LANES

    @pl.when(step == 0)
    def _():
        s_ref[...] = jnp.zeros_like(s_ref)

    rows = []
    for b in range(bsz):
        xb = x_ref[b]
        _fill_halo(buf_ref.at[b], xb, xp_ref[b], xn_ref[b], blk * c, c, ctx_len, total)
        rows.append(xb + (0.5 * (_shifted(buf_ref.at[b], -1, c) + _shifted(buf_ref.at[b], 1, c)) - xb) * mu_ref[...])
    x = jnp.concatenate(rows, axis=0)
    n = bsz * c
    r = x[:, 0:bw]
    k = x[:, bw:2 * bw]
    v = x[:, 2 * bw:3 * bw]
    off = 3 * bw
    gd = x[:, off:off + RWKV_GATE_RANK]
    off += RWKV_GATE_RANK
    wd = [x[:, off + d * RWKV_RANK:off + (d + 1) * RWKV_RANK] for d in range(N_DIR)]
    off += N_DIR * RWKV_RANK
    ad = [x[:, off + d * RWKV_RANK:off + (d + 1) * RWKV_RANK] for d in range(N_DIR)]

    def per_head_sum(t):
        t1, t2, t3 = _split3(t)
        hs = hs_ref[...]
        return (jnp.dot(t1, hs, preferred_element_type=F32) + jnp.dot(t2, hs, preferred_element_type=F32)
                + jnp.dot(t3, hs, preferred_element_type=F32))

    def a_of(d):
        return _sigmoid(a0_ref[d:d + 1, :] + _mm(ad[d], a2_ref[d]))

    d = int(reverse)
    w_raw = -_softplus(-(w0_ref[d:d + 1, :] + _mm(jnp.tanh(wd[d]), w2_ref[d]))) - 0.5
    logw = -jnp.exp(w_raw)
    a_dir = a_of(d)
    kkv = k * kk_ref[...]
    kkv = kkv * lax.rsqrt(per_head_sum(kkv * kkv) + EPS)
    kd = k * (1.0 + (a_dir - 1.0) * ka_ref[...])

    cshift = c.bit_length() - 1

    def chunk_masks(size):
        ii = _iota2((size, size), 0)
        jj = _iota2((size, size), 1)
        same = (ii >> cshift) == (jj >> cshift)
        return ((ii <= jj) if reverse else (ii >= jj)) & same, ((ii < jj) if reverse else (ii > jj)) & same

    cum = _tri_mm(jnp.where(chunk_masks(n)[0], 1.0, 0.0), logw)
    last = 0 if reverse else c - 1
    cum_tot = jnp.concatenate([jnp.broadcast_to(cum[b * c + last:b * c + last + 1, :], (c, bw))
                               for b in range(bsz)], axis=0)
    w_inv = jnp.exp(-cum)
    w_rest = jnp.exp(cum_tot - cum)
    alpha_t = -kkv * jnp.exp(cum - logw)
    beta = kkv * a_dir
    beta_t = beta * w_inv
    k_t = kd * w_inv
    r_t = r * jnp.exp(cum)
    beta_h = beta * w_rest
    k_h = kd * w_rest

    n2 = 2 * c
    incl, strict = chunk_masks(n2)
    lane = _iota2((1, LANES), 1)
    m0 = jnp.where(lane < RWKV_N, 1.0, 0.0)
    m1 = 1.0 - m0
    incl2 = jnp.concatenate([incl, incl], axis=1)

    def stack(t, b, p):
        t = t[b * c:(b + 1) * c, p * LANES:(p + 1) * LANES]
        return jnp.concatenate([t * m0, t * m1], axis=0)

    chains = [(b, p) for b in range(bsz) for p in range(n_pairs)]
    states = [s_ref[b * n_pairs + p] for b, p in chains]
    ar = [jnp.concatenate([stack(alpha_t, b, p), stack(r_t, b, p)], axis=0) for b, p in chains]
    bk = [jnp.concatenate([stack(beta_t, b, p), stack(k_t, b, p)], axis=0) for b, p in chains]
    vs = [stack(v, b, p) for b, p in chains]
    bkh = [jnp.concatenate([stack(beta_h, b, p), stack(k_h, b, p)], axis=0) for b, p in chains]
    g = [_mm_nt(x1, x2) for x1, x2 in zip(ar, bk)]
    t_invs = _tri_inverse([jnp.where(strict, -m[0:n2, 0:n2], 0.0) for m in g], c)
    mv = [_mm(jnp.where(strict, m[0:n2, n2:2 * n2], 0.0), vv) for m, vv in zip(g, vs)]
    xs = [_mm_nt(x1, st) for x1, st in zip(ar, states)]
    us = [_mm(t, x1[0:n2] + m) for t, x1, m in zip(t_invs, xs, mv)]
    uv = [jnp.concatenate([u, vv], axis=0) for u, vv in zip(us, vs)]
    ys = [x1[n2:2 * n2] + _mm(jnp.where(incl2, m[n2:2 * n2, :], 0.0), w) for x1, m, w in zip(xs, g, uv)]
    for i, (b, p) in enumerate(chains):
        w_tot = jnp.exp(cum_tot[b * c:b * c + 1, p * LANES:(p + 1) * LANES])
        s_ref[b * n_pairs + p] = states[i] * w_tot + _mm_tn(uv[i], bkh[i])
    y = jnp.concatenate([jnp.concatenate([ys[b * n_pairs + p][0:c] + ys[b * n_pairs + p][c:n2]
                                          for p in range(n_pairs)], axis=1) for b in range(bsz)], axis=0)
    if reverse:
        y = y + jnp.concatenate([yf_ref[b] for b in range(bsz)], axis=0)
        mean = per_head_sum(y) * (1.0 / RWKV_N)
        yc = y - mean
        var = per_head_sum(yc * yc) * (1.0 / RWKV_N)
        yn = yc * lax.rsqrt(var + RWKV_LN_EPS) * lng_ref[...] + lnb_ref[...]
        k_sum = kd + k * (1.0 + (a_of(0) - 1.0) * ka_ref[...])
        bonus = per_head_sum(r * k_sum * rk_ref[...]) * v
        y = (yn + bonus) * _mm(_sigmoid(gd), g2_ref[...])
    for b in range(bsz):
        o_ref[b] = y[b * c:(b + 1) * c]


def _rwkv7(p, mu, w0, w2, a0, a2, g2, k_k, k_a, r_k, ln_g, ln_b, ctx_len):
    bsz, total, pw = p.shape
    bw = k_k.shape[0]
    c = RWKV_CHUNK
    n_all, n_ctx = total // c, ctx_len // c
    per = c // HALO
    n8 = total // HALO
    head_id = jnp.arange(bw) // RWKV_N
    head_sum = (head_id[:, None] == head_id[None, :]).astype(BF16)
    y = None
    for reverse in (False, True):
        def blk(s, reverse=reverse):
            return (0, _scan_block(s, n_ctx, n_all, reverse), 0)

        def prev(s, reverse=reverse):
            return (0, jnp.maximum(_scan_block(s, n_ctx, n_all, reverse) * per - 1, 0), 0)

        def nxt(s, reverse=reverse):
            return (0, jnp.minimum((_scan_block(s, n_ctx, n_all, reverse) + 1) * per, n8 - 1), 0)

        in_specs = [pl.BlockSpec((bsz, c, pw), blk),
                    pl.BlockSpec((bsz, HALO, pw), prev),
                    pl.BlockSpec((bsz, HALO, pw), nxt),
                    _const_spec((1, pw)),
                    _const_spec((N_DIR, bw)), _const_spec((N_DIR, RWKV_RANK, bw)),
                    _const_spec((N_DIR, bw)), _const_spec((N_DIR, RWKV_RANK, bw)),
                    _const_spec((1, bw)), _const_spec((1, bw)), _const_spec((bw, bw))]
        args = [p, p, p, mu.reshape(1, pw), w0, w2.astype(BF16), a0, a2.astype(BF16),
                k_k.reshape(1, bw), k_a.reshape(1, bw), head_sum]
        if reverse:
            in_specs += [_const_spec((RWKV_GATE_RANK, bw)), _const_spec((1, bw)), _const_spec((1, bw)),
                         _const_spec((1, bw)), pl.BlockSpec((bsz, c, bw), blk)]
            args += [g2.astype(BF16), r_k.reshape(1, bw), ln_g.reshape(1, bw), ln_b.reshape(1, bw), y]
        y = pl.pallas_call(
            functools.partial(_rwkv_kernel, reverse=reverse, ctx_len=ctx_len, total=total,
                              n_ctx=n_ctx, n_all=n_all),
            grid=(n_all,),
            in_specs=in_specs,
            out_specs=pl.BlockSpec((bsz, c, bw), blk),
            out_shape=jax.ShapeDtypeStruct((bsz, total, bw), F32),
            scratch_shapes=[pltpu.VMEM((bsz, c + 2 * HALO, pw), F32),
                            pltpu.VMEM((bsz * (bw // LANES), LANES, LANES), F32)],
            compiler_params=_params(1),
            name="rwkv_bwd" if reverse else "rwkv_fwd",
        )(*args)
    return y


def _merge_kernel(h_ref, xm_ref, mod_ref, wg_ref, gb_ref, y0_ref, y1_ref, y2_ref, y3_ref, bw_ref, ow_ref,
                  o_ref, *, rows, ctx_len):
    tile = pl.program_id(1)
    h = h_ref[0]
    d = h.shape[1]
    xm = xm_ref[0]
    merged = jnp.zeros_like(h)
    for k, y_ref in enumerate((y0_ref, y1_ref, y2_ref, y3_ref)):
        gate = _sigmoid(jnp.dot(xm, wg_ref[:, k * d:(k + 1) * d], preferred_element_type=F32)
                        + gb_ref[:, k * d:(k + 1) * d])
        merged = merged + gate * _mm(y_ref[0], bw_ref[k])
    o_ref[0] = h + _mod_row(mod_ref, tile, rows, ctx_len, 2) * _mm(merged, ow_ref[...])


def _merge(h, xm, modtab, w_gates, gate_b, ys, branch_w, out_w, ctx_len):
    bsz, total, d = h.shape
    rows = _token_tile(total, TOKEN_TILES)
    bwid = ys[0].shape[2]
    row_spec = lambda width: pl.BlockSpec((1, rows, width), lambda b, i: (b, i, 0))
    return pl.pallas_call(
        functools.partial(_merge_kernel, rows=rows, ctx_len=ctx_len),
        grid=(bsz, total // rows),
        in_specs=[row_spec(d), row_spec(d),
                  pl.BlockSpec((1, 2, 6, d), lambda b, i: (b, 0, 0, 0)),
                  _weight_spec((d, N_BRANCH * d)), _const_spec((1, N_BRANCH * d)),
                  row_spec(bwid), row_spec(bwid), row_spec(bwid), row_spec(bwid),
                  _weight_spec((N_BRANCH, bwid, d)), _weight_spec((d, d))],
        out_specs=row_spec(d),
        out_shape=jax.ShapeDtypeStruct((bsz, total, d), F32),
        compiler_params=_params(),
        name="merge",
    )(h, xm, modtab, w_gates, gate_b.reshape(1, -1), *ys, branch_w, out_w)


def _ffn_kernel(h_ref, g_ref, mod_ref, w1_ref, w3_ref, w2_ref, fg_ref, o_ref, *, rows, ctx_len, skip, final):
    tile = pl.program_id(1) + skip
    h = h_ref[0]
    u = _norm_modulate(h, g_ref[...], mod_ref, tile, rows, ctx_len, 3, 4).astype(BF16)
    dff = w1_ref.shape[1]
    part = dff // FFN_SPLIT
    out = jnp.zeros_like(h)
    for j in range(FFN_SPLIT):
        cols = slice(j * part, (j + 1) * part)
        t = (_silu(jnp.dot(u, w1_ref[:, cols], preferred_element_type=F32))
             * jnp.dot(u, w3_ref[:, cols], preferred_element_type=F32))
        out = out + _mm(t, w2_ref[cols, :])
    h = h + _mod_row(mod_ref, tile, rows, ctx_len, 5) * out
    if final:
        h = h * lax.rsqrt(jnp.mean(h * h, axis=-1, keepdims=True) + EPS) * fg_ref[...]
    o_ref[0] = h


def _ffn(h, g, modtab, w1, w3, w2, final_g, ctx_len, final):
    bsz, total, d = h.shape
    rows = _token_tile(ctx_len, FFN_TILES)
    dff = w1.shape[1]
    skip = ctx_len // rows if final else 0
    out_rows = total - skip * rows
    return pl.pallas_call(
        functools.partial(_ffn_kernel, rows=rows, ctx_len=ctx_len, skip=skip, final=final),
        grid=(bsz, out_rows // rows),
        in_specs=[pl.BlockSpec((1, rows, d), lambda b, i: (b, i + skip, 0)), _const_spec((1, d)),
                  pl.BlockSpec((1, 2, 6, d), lambda b, i: (b, 0, 0, 0)),
                  _weight_spec((d, dff)), _weight_spec((d, dff)), _weight_spec((dff, d)), _const_spec((1, d))],
        out_specs=pl.BlockSpec((1, rows, d), lambda b, i: (b, i, 0)),
        out_shape=jax.ShapeDtypeStruct((bsz, out_rows, d), F32),
        compiler_params=_params(),
        name="ffn",
    )(h, g.reshape(1, d), modtab, w1, w3, w2, final_g.reshape(1, d))


def _rope_tables(seq, ctx_len):
    t = jnp.arange(seq)
    row = (t // GRID_W).astype(F32)
    col = (t % GRID_W).astype(F32)
    quarter = HEAD_DK // 4
    inv = ROPE_BASE ** (-jnp.arange(quarter, dtype=F32) / quarter)
    ang = jnp.concatenate([row[:, None] * inv, col[:, None] * inv], axis=-1)
    cos, sin = jnp.cos(ang), jnp.sin(ang)
    cos = jnp.concatenate([cos, cos], axis=-1)
    sin = jnp.concatenate([-sin, sin], axis=-1)
    cos = jnp.concatenate([jnp.ones((ctx_len, HEAD_DK), F32), cos], axis=0)
    sin = jnp.concatenate([jnp.zeros((ctx_len, HEAD_DK), F32), sin], axis=0)
    return cos, sin


def kernel(x, c, ctx, c_ctx, mod_w, mod_b, norm1_g, norm2_g, in_w, gate_b, ret_decay_exp, lru_conv_w,
           lru_conv_b, lru_gate_w, lru_gate_b, lru_lambda, gdn_conv_w, gdn_a_log, gdn_dt_bias, gdn_norm_g,
           rwkv_mu, rwkv_w0, rwkv_w2, rwkv_a0, rwkv_a2, rwkv_g2, rwkv_k_k, rwkv_k_a, rwkv_r_k, rwkv_ln_g,
           rwkv_ln_b, branch_w, out_w, ffn_w1, ffn_w3, ffn_w2, final_norm_g):
    bsz, seq, d = x.shape
    ctx_len = ctx.shape[1]
    depth = mod_w.shape[0]
    bw = d // 2
    assert ctx_len % LRU_BLOCK_ROWS == 0 and seq % LRU_BLOCK_ROWS == 0 and bsz + 1 <= HALO

    h = jnp.concatenate([ctx, x], axis=1)
    cond = jnp.zeros((HALO, d), F32).at[:bsz].set(c).at[bsz].set(c_ctx)
    mods = _modulation(cond, mod_w, mod_b).reshape(depth, HALO, 6, d)
    cos, sin = _rope_tables(seq, ctx_len)

    widths = (4 * bw, 2 * bw, 3 * bw, bw, N_DIR * N_HEADS, N_DIR * N_HEADS,
              3 * bw + RWKV_GATE_RANK + 2 * N_DIR * RWKV_RANK, N_BRANCH * d)
    offs = [0]
    for wdt in widths:
        offs.append(offs[-1] + wdt)

    for l in range(depth):
        lat = mods[l, :bsz]
        modtab = jnp.stack([jnp.broadcast_to(mods[l, bsz], lat.shape), lat], axis=1)
        wl = in_w[l]
        w_ret = wl[:, offs[0]:offs[1]].astype(BF16)
        w_lru = wl[:, offs[1]:offs[2]].astype(BF16)
        w_ab = wl[:, offs[4]:offs[6]]
        w_gdn = jnp.concatenate([wl[:, offs[2]:offs[4]], w_ab,
                                 jnp.zeros((d, LANES - w_ab.shape[1]), F32)], axis=1).astype(BF16)
        w_abt = w_ab.T.astype(BF16)
        w_rwkv = wl[:, offs[6]:offs[7]].astype(BF16)
        w_gates = wl[:, offs[7]:offs[8]].astype(BF16)

        xm = _mixer_input(h, norm1_g[l], modtab, ctx_len)
        p_ret = _in_projection(xm, w_ret)
        p_lru = _in_projection(xm, w_lru)
        p_gdn, abt = _in_projection(xm, w_gdn, w_t=w_abt)
        p_rwkv = _in_projection(xm, w_rwkv)

        log_gamma = jnp.log1p(-jnp.exp2(-ret_decay_exp[l].astype(F32)))
        lg_rows = jnp.repeat(log_gamma, HEAD_DK, axis=1).reshape(N_DIR, 1, N_HEADS * HEAD_DK)
        y_ret = _retention(p_ret, cos, sin, lg_rows, ctx_len)
        y_lru = _rglru(p_lru, lru_conv_w[l], lru_conv_b[l], lru_gate_w[l], lru_gate_b[l], lru_lambda[l], ctx_len)
        y_gdn = _gated_deltanet(p_gdn, abt, gdn_conv_w[l], gdn_a_log[l], gdn_dt_bias[l], gdn_norm_g[l], ctx_len)
        y_rwkv = _rwkv7(p_rwkv, rwkv_mu[l], rwkv_w0[l], rwkv_w2[l], rwkv_a0[l], rwkv_a2[l], rwkv_g2[l],
                        rwkv_k_k[l], rwkv_k_a[l], rwkv_r_k[l].reshape(-1), rwkv_ln_g[l], rwkv_ln_b[l], ctx_len)

        h = _merge(h, xm, modtab, w_gates, gate_b[l], (y_ret, y_lru, y_gdn, y_rwkv),
                   branch_w[l].astype(BF16), out_w[l].astype(BF16), ctx_len)
        h = _ffn(h, norm2_g[l], modtab, ffn_w1[l].astype(BF16), ffn_w3[l].astype(BF16),
                 ffn_w2[l].astype(BF16), final_norm_g, ctx_len, final=(l == depth - 1))
    return h
```

```python
import functools

import jax
import jax.numpy as jnp
from jax import lax
from jax.experimental import pallas as pl
from jax.experimental.pallas import tpu as pltpu

F32 = jnp.float32
BF16 = jnp.bfloat16

GRID_W = 64
ROPE_BASE = 10000.0
EPS = 1e-6
RWKV_LN_EPS = 64e-5
LRU_C = 8.0
N_DIR = 2
N_BRANCH = 4
HEAD_DK = 128
N_HEADS = 4
RWKV_N = 64
RWKV_RANK = 64
RWKV_GATE_RANK = 128
CONV_K = 4
LANES = 128
HALO = 8
VMEM_LIMIT_BYTES = 56 * 1024 * 1024

TOKEN_TILES = (768, 512, 256)
PROJ_TILES = (1408,) + TOKEN_TILES
RET_CHUNK = 128
LRU_BLOCK_ROWS = 256
GDN_CHUNK = 128
RWKV_CHUNK = 64
FFN_TILES = (256,)
FFN_SPLIT = 1


def _mm(a, b):
    return jnp.dot(a.astype(BF16), b.astype(BF16), preferred_element_type=F32)


def _mm_nt(a, b):
    return lax.dot_general(a.astype(BF16), b.astype(BF16), (((1,), (1,)), ((), ())),
                           preferred_element_type=F32)


def _mm_tn(a, b):
    return lax.dot_general(a.astype(BF16), b.astype(BF16), (((0,), (0,)), ((), ())),
                           preferred_element_type=F32)


def _split3(x):
    x1 = x.astype(BF16)
    r1 = x - x1.astype(F32)
    x2 = r1.astype(BF16)
    x3 = (r1 - x2.astype(F32)).astype(BF16)
    return x1, x2, x3


def _tri_mm(tri, x):
    x1, x2, x3 = _split3(x)
    t = tri.astype(BF16)
    return (jnp.dot(t, x1, preferred_element_type=F32) + jnp.dot(t, x2, preferred_element_type=F32)
            + jnp.dot(t, x3, preferred_element_type=F32))


def _mm_tri(x, tri):
    x1, x2, x3 = _split3(x)
    t = tri.astype(BF16)
    return (jnp.dot(x1, t, preferred_element_type=F32) + jnp.dot(x2, t, preferred_element_type=F32)
            + jnp.dot(x3, t, preferred_element_type=F32))


def _sigmoid(x):
    return 1.0 / (1.0 + jnp.exp(-x))


def _silu(x):
    return x * _sigmoid(x)


def _softplus(x):
    return jnp.maximum(x, 0.0) + jnp.log1p(jnp.exp(-jnp.abs(x)))


def _gelu_tanh(x):
    return 0.5 * x * (1.0 + jnp.tanh(0.7978845608028654 * (x + 0.044715 * x * x * x)))


def _iota2(shape, axis):
    return lax.broadcasted_iota(jnp.int32, shape, axis)


def _scan_block(s, n_ctx, n_all, reverse):
    if not reverse:
        return s
    return jnp.where(s < n_ctx, n_ctx - 1 - s, n_all - 1 - (s - n_ctx))


def _tri_inverse(mats, n_max):
    n = mats[0].shape[0]
    ii = _iota2((n, n), 0)
    jj = _iota2((n, n), 1)
    eye = jnp.where(ii == jj, 1.0, 0.0)
    pair = (ii >> 1) == (jj >> 1)
    ts = [eye - jnp.where(pair, a, 0.0) for a in mats]
    b = 2
    while b < n_max:
        sh = b.bit_length() - 1
        between = ((ii >> (sh + 1)) == (jj >> (sh + 1))) & ((ii >> sh) != (jj >> sh))
        xs = [_mm(t, jnp.where(between, a, 0.0)) for t, a in zip(ts, mats)]
        ts = [t - _mm(x, t) for t, x in zip(ts, xs)]
        b *= 2
    return ts


def _fill_halo(buf_ref, x, prev, nxt, row0, rows, ctx_len, total):
    prev_ok = jnp.logical_and(row0 != 0, row0 != ctx_len)
    next_ok = jnp.logical_and(row0 + rows != ctx_len, row0 + rows != total)
    buf_ref[0:HALO, :] = jnp.where(prev_ok, prev, 0.0)
    buf_ref[HALO:HALO + rows, :] = x
    buf_ref[HALO + rows:HALO + rows + HALO, :] = jnp.where(next_ok, nxt, 0.0)


def _shifted(buf_ref, off, rows):
    return buf_ref[HALO + off:HALO + off + rows, :]


def _shift_conv(x, prev, nxt, taps_ref, row0, ctx_len, total):
    rows = x.shape[0]
    centre = CONV_K // 2
    offs = [j - centre for j in range(CONV_K) if j != centre]
    prev = jnp.where(jnp.logical_and(row0 != 0, row0 != ctx_len), prev, 0.0)
    nxt = jnp.where(jnp.logical_and(row0 + rows != ctx_len, row0 + rows != total), nxt, 0.0)
    ii = _iota2((rows, 2 * rows), 0)
    jj = _iota2((rows, 2 * rows), 1) & (rows - 1)
    shifts = jnp.concatenate([jnp.where(jj == ii + o, 1.0, 0.0) for o in offs], axis=0).astype(BF16)
    hi = x.astype(BF16)
    lo = (x - hi.astype(F32)).astype(BF16)
    moved = jnp.dot(shifts, jnp.concatenate([hi, lo], axis=0), preferred_element_type=F32)
    out = taps_ref[centre:centre + 1, :] * x
    r8 = _iota2((HALO, 1), 0)
    head = jnp.zeros_like(prev)
    tail = jnp.zeros_like(nxt)
    for n, o in enumerate(offs):
        tap = taps_ref[o + centre:o + centre + 1, :]
        out = out + tap * moved[n * rows:(n + 1) * rows]
        if o < 0:
            head = head + tap * jnp.where(r8 < -o, pltpu.roll(prev, -o, 0), 0.0)
        else:
            tail = tail + tap * jnp.where(r8 >= HALO - o, pltpu.roll(nxt, HALO - o, 0), 0.0)
    return jnp.concatenate([out[:HALO] + head, out[HALO:rows - HALO], out[rows - HALO:] + tail], axis=0)


def _norm_modulate(x, g, mod_ref, tile, rows, ctx_len, shift_idx, scale_idx):
    xn = x * lax.rsqrt(jnp.mean(x * x, axis=-1, keepdims=True) + EPS) * g
    is_ctx = (tile * rows + _iota2((rows, 1), 0)) < ctx_len
    shift = jnp.where(is_ctx, mod_ref[0, 0, shift_idx:shift_idx + 1, :], mod_ref[0, 1, shift_idx:shift_idx + 1, :])
    scale = jnp.where(is_ctx, mod_ref[0, 0, scale_idx:scale_idx + 1, :], mod_ref[0, 1, scale_idx:scale_idx + 1, :])
    return xn * (1.0 + scale) + shift


def _mod_row(mod_ref, tile, rows, ctx_len, idx):
    is_ctx = (tile * rows + _iota2((rows, 1), 0)) < ctx_len
    return jnp.where(is_ctx, mod_ref[0, 0, idx:idx + 1, :], mod_ref[0, 1, idx:idx + 1, :])


def _params(n_axes=2):
    return pltpu.CompilerParams(dimension_semantics=("arbitrary",) * n_axes,
                                vmem_limit_bytes=VMEM_LIMIT_BYTES)


def _const_spec(shape):
    nd = len(shape)
    return pl.BlockSpec(shape, lambda *_: (0,) * nd)


def _weight_spec(shape):
    nd = len(shape)
    return pl.BlockSpec(shape, lambda *_: (0,) * nd, pipeline_mode=pl.Buffered(1))


def _token_tile(total, candidates):
    for rows in candidates:
        if total % rows == 0:
            return rows
    raise ValueError(f"token count {total} is not a multiple of {candidates[-1]}")


def _mod_kernel(c_ref, w_ref, b_ref, o_ref):
    o_ref[0] = _mm(_silu(c_ref[...]), w_ref[0]) + b_ref[0]


def _modulation(cond, mod_w, mod_b):
    depth, d, n = mod_w.shape
    tn = n // 4
    return pl.pallas_call(
        _mod_kernel,
        grid=(depth, n // tn),
        in_specs=[pl.BlockSpec(cond.shape, lambda l, j: (0, 0)),
                  pl.BlockSpec((1, d, tn), lambda l, j: (l, 0, j)),
                  pl.BlockSpec((1, 1, tn), lambda l, j: (l, 0, j))],
        out_specs=pl.BlockSpec((1, cond.shape[0], tn), lambda l, j: (l, 0, j)),
        out_shape=jax.ShapeDtypeStruct((depth, cond.shape[0], n), F32),
        compiler_params=_params(),
        name="modulation",
    )(cond, mod_w, mod_b.reshape(depth, 1, n))


def _normmod_kernel(h_ref, g_ref, mod_ref, o_ref, *, rows, ctx_len):
    o_ref[0] = _norm_modulate(h_ref[0], g_ref[...], mod_ref, pl.program_id(1), rows, ctx_len, 0, 1).astype(BF16)


def _mixer_input(h, g, modtab, ctx_len):
    bsz, total, d = h.shape
    rows = _token_tile(total, TOKEN_TILES)
    row_spec = pl.BlockSpec((1, rows, d), lambda b, i: (b, i, 0))
    return pl.pallas_call(
        functools.partial(_normmod_kernel, rows=rows, ctx_len=ctx_len),
        grid=(bsz, total // rows),
        in_specs=[row_spec, _const_spec((1, d)), pl.BlockSpec((1, 2, 6, d), lambda b, i: (b, 0, 0, 0))],
        out_specs=row_spec,
        out_shape=jax.ShapeDtypeStruct((bsz, total, d), BF16),
        compiler_params=_params(),
        name="norm_modulate",
    )(h, g.reshape(1, d), modtab)


def _inproj_kernel(x_ref, w_ref, *rest, with_t):
    if with_t:
        wt_ref, o_ref, ot_ref = rest
    else:
        (o_ref,) = rest
    o_ref[0] = jnp.dot(x_ref[0], w_ref[...], preferred_element_type=F32)
    if with_t:
        ot_ref[0] = lax.dot_general(wt_ref[...], x_ref[0], (((1,), (1,)), ((), ())),
                                    preferred_element_type=F32)


def _in_projection(xm, w, w_t=None):
    bsz, total, d = xm.shape
    n = w.shape[1]
    rows = _token_tile(total, PROJ_TILES)
    in_specs = [pl.BlockSpec((1, rows, d), lambda b, i: (b, i, 0)),
                _weight_spec((d, n))]
    out_specs = [pl.BlockSpec((1, rows, n), lambda b, i: (b, i, 0))]
    out_shape = [jax.ShapeDtypeStruct((bsz, total, n), F32)]
    args = [xm, w]
    if w_t is not None:
        nt = w_t.shape[0]
        in_specs.append(_const_spec((nt, d)))
        out_specs.append(pl.BlockSpec((1, nt, rows), lambda b, i: (b, 0, i)))
        out_shape.append(jax.ShapeDtypeStruct((bsz, nt, total), F32))
        args.append(w_t)
    out = pl.pallas_call(
        functools.partial(_inproj_kernel, with_t=w_t is not None),
        grid=(bsz, total // rows),
        in_specs=in_specs, out_specs=out_specs, out_shape=out_shape,
        compiler_params=_params(),
        name="in_projection",
    )(*args)
    return out if w_t is not None else out[0]


def _retention_kernel(p_ref, cos_ref, sin_ref, lg_ref, *rest, reverse):
    if reverse:
        yf_ref, o_ref, s_ref, dmat_ref, qdec_ref, kdec_ref = rest
    else:
        o_ref, s_ref, dmat_ref, qdec_ref, kdec_ref = rest
    c = RET_CHUNK
    bsz = p_ref.shape[0]
    lg = lg_ref[...]

    @pl.when(pl.program_id(0) == 0)
    def _():
        s_ref[...] = jnp.zeros_like(s_ref)
        pos = _iota2((c, 1), 0).astype(F32)
        qdec_ref[...] = jnp.exp(((c - pos) if reverse else (pos + 1.0)) * lg)
        kdec_ref[...] = jnp.exp((pos if reverse else (c - 1.0 - pos)) * lg)
        ii = _iota2((c, c), 0)
        jj = _iota2((c, c), 1)
        rel = (jj - ii) if reverse else (ii - jj)
        keep = rel >= 0
        relf = jnp.where(keep, rel, 0).astype(F32)
        for h in range(N_HEADS):
            dmat_ref[h] = jnp.where(keep, jnp.exp(relf * lg[:, h * HEAD_DK:(h + 1) * HEAD_DK]), 0.0)

    cos = cos_ref[...]
    sin = sin_ref[...]
    w = N_HEADS * HEAD_DK
    chains = [(b, h) for b in range(bsz) for h in range(N_HEADS)]
    states = [s_ref[b * N_HEADS + h] for b, h in chains]
    qs, ks, vs = [], [], []
    for b, h in chains:
        sl = slice(h * HEAD_DK, (h + 1) * HEAD_DK)
        q = p_ref[b, :, sl]
        k = p_ref[b, :, w + h * HEAD_DK:w + (h + 1) * HEAD_DK] * (HEAD_DK ** -0.5)
        qs.append(q * cos + pltpu.roll(q, HEAD_DK // 2, 1) * sin)
        ks.append(k * cos + pltpu.roll(k, HEAD_DK // 2, 1) * sin)
        vs.append(p_ref[b, :, 2 * w + h * HEAD_DK:2 * w + (h + 1) * HEAD_DK])
    scores = [_mm_nt(q, k) * dmat_ref[h] for q, k, (b, h) in zip(qs, ks, chains)]
    ys = [_mm(jnp.concatenate([sc, q * qdec_ref[:, h * HEAD_DK:(h + 1) * HEAD_DK]], axis=1),
              jnp.concatenate([v, st], axis=0))
          for sc, q, v, st, (b, h) in zip(scores, qs, vs, states, chains)]
    for i, (b, h) in enumerate(chains):
        sl = slice(h * HEAD_DK, (h + 1) * HEAD_DK)
        s_ref[b * N_HEADS + h] = (states[i] * jnp.exp(c * lg[:, sl])
                                  + _mm_tn(ks[i] * kdec_ref[:, sl], vs[i]))
        y = ys[i]
        if reverse:
            y = y + yf_ref[b, :, sl]
            g = p_ref[b, :, 3 * w + h * HEAD_DK:3 * w + (h + 1) * HEAD_DK]
            y = y * lax.rsqrt(jnp.mean(y * y, axis=-1, keepdims=True) + EPS) * _silu(g)
        o_ref[b, :, sl] = y


def _retention(p, cos, sin, log_gamma_rows, ctx_len):
    bsz, total, _ = p.shape
    w = N_HEADS * HEAD_DK
    c = RET_CHUNK
    n_all, n_ctx = total // c, ctx_len // c
    y = None
    for reverse in (False, True):
        def blk(s, reverse=reverse):
            return (0, _scan_block(s, n_ctx, n_all, reverse), 0)

        def tab(s, reverse=reverse):
            return (_scan_block(s, n_ctx, n_all, reverse), 0)

        in_specs = [pl.BlockSpec((bsz, c, 4 * w), blk),
                    pl.BlockSpec((c, HEAD_DK), tab),
                    pl.BlockSpec((c, HEAD_DK), tab),
                    _const_spec((1, w))]
        args = [p, cos, sin, log_gamma_rows[int(reverse)]]
        if reverse:
            in_specs.append(pl.BlockSpec((bsz, c, w), blk))
            args.append(y)
        y = pl.pallas_call(
            functools.partial(_retention_kernel, reverse=reverse),
            grid=(n_all,),
            in_specs=in_specs,
            out_specs=pl.BlockSpec((bsz, c, w), blk),
            out_shape=jax.ShapeDtypeStruct((bsz, total, w), F32),
            scratch_shapes=[pltpu.VMEM((bsz * N_HEADS, HEAD_DK, HEAD_DK), F32),
                            pltpu.VMEM((N_HEADS, c, c), F32),
                            pltpu.VMEM((c, w), F32), pltpu.VMEM((c, w), F32)],
            compiler_params=_params(1),
            name="retention_bwd" if reverse else "retention_fwd",
        )(*args)
    return y


def _lru_kernel(x_ref, xp_ref, xn_ref, cw_ref, cb_ref, gw_ref, gb_ref, lam_ref, *rest,
                reverse, ctx_len, total, n_ctx, n_all):
    if reverse:
        hf_ref, o_ref, buf_ref, carry_ref = rest
    else:
        o_ref, buf_ref, carry_ref = rest
    rows = LRU_BLOCK_ROWS
    s = pl.program_id(1)
    blk = _scan_block(s, n_ctx, n_all, reverse)

    @pl.when(s == 0)
    def _():
        carry_ref[...] = jnp.zeros_like(carry_ref)

    w = x_ref.shape[2] // 2
    xc = cb_ref[...] + _shift_conv(x_ref[0, :, 0:w], xp_ref[0], xn_ref[0], cw_ref, blk * rows, ctx_len, total)
    n_blocks = gw_ref.shape[0]
    bw = w // n_blocks
    parts = [_mm(xc[:, n * bw:(n + 1) * bw], gw_ref[n]) for n in range(n_blocks)]
    r_gate = _sigmoid(jnp.concatenate([p[:, :bw] for p in parts], axis=1) + gb_ref[0:1, :])
    i_gate = _sigmoid(jnp.concatenate([p[:, bw:] for p in parts], axis=1) + gb_ref[1:2, :])
    log_a = LRU_C * r_gate * (-_softplus(-lam_ref[...]))
    a = jnp.exp(log_a)
    b = jnp.sqrt(1.0 - jnp.exp(2.0 * log_a)) * i_gate * xc
    n_groups = rows // HALO
    a = a.reshape(n_groups, HALO, w)
    b = b.reshape(n_groups, HALO, w)
    rin = _iota2((1, HALO, 1), 1)
    d = 1
    while d < HALO:
        if reverse:
            a_s, b_s, ok = pltpu.roll(a, HALO - d, 1), pltpu.roll(b, HALO - d, 1), rin < HALO - d
        else:
            a_s, b_s, ok = pltpu.roll(a, d, 1), pltpu.roll(b, d, 1), rin >= d
        b = b + a * jnp.where(ok, b_s, 0.0)
        a = a * jnp.where(ok, a_s, 1.0)
        d *= 2
    carry = carry_ref[...]
    groups = [None] * n_groups
    for g in (range(n_groups - 1, -1, -1) if reverse else range(n_groups)):
        hg = b[g] + a[g] * carry
        carry = hg[0:1, :] if reverse else hg[HALO - 1:HALO, :]
        groups[g] = hg
    h = jnp.concatenate(groups, axis=0)
    carry_ref[...] = carry
    if reverse:
        h = (h + hf_ref[0]) * _gelu_tanh(x_ref[0, :, w:2 * w])
    o_ref[0] = h


def _rglru(p, conv_w, conv_b, gate_w, gate_b, lam, ctx_len):
    bsz, total, w2 = p.shape
    w = w2 // 2
    rows = LRU_BLOCK_ROWS
    n_all, n_ctx = total // rows, ctx_len // rows
    per = rows // HALO
    n8 = total // HALO
    n_blocks, bw = gate_w.shape[2], gate_w.shape[3]
    gw = jnp.transpose(gate_w, (0, 2, 3, 1, 4)).reshape(N_DIR, n_blocks, bw, 2 * bw).astype(BF16)
    h = None
    for reverse in (False, True):
        def blk(b, s, reverse=reverse):
            return (b, _scan_block(s, n_ctx, n_all, reverse), 0)

        def prev(b, s, reverse=reverse):
            return (b, jnp.maximum(_scan_block(s, n_ctx, n_all, reverse) * per - 1, 0), 0)

        def nxt(b, s, reverse=reverse):
            return (b, jnp.minimum((_scan_block(s, n_ctx, n_all, reverse) + 1) * per, n8 - 1), 0)

        d = int(reverse)
        in_specs = [pl.BlockSpec((1, rows, w2), blk),
                    pl.BlockSpec((1, HALO, w), prev),
                    pl.BlockSpec((1, HALO, w), nxt),
                    _const_spec((CONV_K, w)), _const_spec((1, w)),
                    _const_spec((n_blocks, bw, 2 * bw)), _const_spec((2, w)), _const_spec((1, w))]
        args = [p, p, p, conv_w, conv_b.reshape(1, w), gw[d], gate_b[d], lam[d].reshape(1, w)]
        if reverse:
            in_specs.append(pl.BlockSpec((1, rows, w), blk))
            args.append(h)
        h = pl.pallas_call(
            functools.partial(_lru_kernel, reverse=reverse, ctx_len=ctx_len, total=total,
                              n_ctx=n_ctx, n_all=n_all),
            grid=(bsz, n_all),
            in_specs=in_specs,
            out_specs=pl.BlockSpec((1, rows, w), blk),
            out_shape=jax.ShapeDtypeStruct((bsz, total, w), F32),
            scratch_shapes=[pltpu.VMEM((rows + 2 * HALO, w), F32), pltpu.VMEM((1, w), F32)],
            compiler_params=_params(),
            name="rglru_bwd" if reverse else "rglru_fwd",
        )(*args)
    return h


def _gdn_kernel(x_ref, xp_ref, xn_ref, abt_ref, cw_ref, alog_r_ref, dtb_r_ref, alog_c_ref, dtb_c_ref,
                ng_ref, *rest, reverse, ctx_len, total, n_ctx, n_all):
    if reverse:
        yf_ref, o_ref, s_ref = rest
    else:
        o_ref, s_ref = rest
    c = GDN_CHUNK
    s = pl.program_id(0)
    blk = _scan_block(s, n_ctx, n_all, reverse)
    bsz = x_ref.shape[0]

    @pl.when(s == 0)
    def _():
        s_ref[...] = jnp.zeros_like(s_ref)

    w = N_HEADS * HEAD_DK
    ii = _iota2((c, c), 0)
    jj = _iota2((c, c), 1)
    incl = (ii <= jj) if reverse else (ii >= jj)
    strict = (ii < jj) if reverse else (ii > jj)
    incl_t = (ii >= jj) if reverse else (ii <= jj)
    last = 0 if reverse else c - 1

    chains = [(b, h) for b in range(bsz) for h in range(N_HEADS)]
    states = [s_ref[b * N_HEADS + h] for b, h in chains]
    qs, ks, vbs, kbes, decays, e_gcs, g_lasts, gccs = [], [], [], [], [], [], [], []
    for b in range(bsz):
        qkv = _silu(_shift_conv(x_ref[b, :, 0:3 * w], xp_ref[b], xn_ref[b], cw_ref, blk * c, ctx_len, total))
        ab = x_ref[b, :, 4 * w:4 * w + LANES]
        abt = abt_ref[b]
        g_col = -jnp.exp(alog_r_ref[...]) * _softplus(ab + dtb_r_ref[...])
        g_row = -jnp.exp(alog_c_ref[...]) * _softplus(abt + dtb_c_ref[...])
        beta_col = _sigmoid(ab)
        gc_col = _tri_mm(jnp.where(incl, 1.0, 0.0), g_col)
        gc_row = _mm_tri(g_row, jnp.where(incl_t, 1.0, 0.0))
        for h in range(N_HEADS):
            col = int(reverse) * N_HEADS + h
            gcc = gc_col[:, col:col + 1]
            gcr = gc_row[col:col + 1, :]
            beta = beta_col[:, 2 * N_HEADS + col:2 * N_HEADS + col + 1]
            decays.append(jnp.where(incl, jnp.exp(jnp.where(incl, gcc - gcr, 0.0)), 0.0))
            q = qkv[:, h * HEAD_DK:(h + 1) * HEAD_DK]
            k = qkv[:, w + h * HEAD_DK:w + (h + 1) * HEAD_DK]
            v = qkv[:, 2 * w + h * HEAD_DK:2 * w + (h + 1) * HEAD_DK]
            qs.append(q * lax.rsqrt(jnp.sum(q * q, axis=-1, keepdims=True) + EPS) * (HEAD_DK ** -0.5))
            k = k * lax.rsqrt(jnp.sum(k * k, axis=-1, keepdims=True) + EPS)
            ks.append(k)
            kbes.append(k * beta)
            vbs.append(v * beta)
            gccs.append(gcc)
            e_gcs.append(jnp.exp(gcc))
            g_lasts.append(gcc[last:last + 1, :])

    kq = [_mm_nt(jnp.concatenate([kb, q], axis=0), k) for kb, q, k in zip(kbes, qs, ks)]
    t_invs = _tri_inverse([jnp.where(strict, m[0:c] * dec, 0.0) for m, dec in zip(kq, decays)], c)
    qks = [m[c:2 * c] * dec for m, dec in zip(kq, decays)]
    uw = [_mm(t, jnp.concatenate([vb, kb * e], axis=1)) for t, vb, kb, e in zip(t_invs, vbs, kbes, e_gcs)]
    ws = [_mm(jnp.concatenate([m[:, HEAD_DK:], q * e], axis=0), st)
          for m, q, e, st in zip(uw, qs, e_gcs, states)]
    v_news = [m[:, :HEAD_DK] - x[0:c] for m, x in zip(uw, ws)]
    ys = [x[c:2 * c] + _mm(qk, vn) for x, qk, vn in zip(ws, qks, v_news)]
    new_states = [st * jnp.exp(gl) + _mm_tn(k * jnp.exp(gl - gcc), vn)
                  for st, gl, k, gcc, vn in zip(states, g_lasts, ks, gccs, v_news)]
    for i, (b, h) in enumerate(chains):
        s_ref[b * N_HEADS + h] = new_states[i]
        y = ys[i]
        if reverse:
            y = y + yf_ref[b, :, h * HEAD_DK:(h + 1) * HEAD_DK]
            z = x_ref[b, :, 3 * w + h * HEAD_DK:3 * w + (h + 1) * HEAD_DK]
            y = y * lax.rsqrt(jnp.mean(y * y, axis=-1, keepdims=True) + EPS) * ng_ref[...] * _silu(z)
        o_ref[b, :, h * HEAD_DK:(h + 1) * HEAD_DK] = y


def _gated_deltanet(p, abt, conv_w, a_log, dt_bias, norm_g, ctx_len):
    bsz, total, pw = p.shape
    w = N_HEADS * HEAD_DK
    c = GDN_CHUNK
    n_all, n_ctx = total // c, ctx_len // c
    per = c // HALO
    n8 = total // HALO
    nt = abt.shape[1]
    flat_alog = a_log.reshape(-1)
    flat_dtb = dt_bias.reshape(-1)
    alog_r = jnp.zeros((1, LANES), F32).at[0, :flat_alog.shape[0]].set(flat_alog)
    dtb_r = jnp.zeros((1, LANES), F32).at[0, :flat_dtb.shape[0]].set(flat_dtb)
    alog_c = jnp.zeros((nt, 1), F32).at[:flat_alog.shape[0], 0].set(flat_alog)
    dtb_c = jnp.zeros((nt, 1), F32).at[:flat_dtb.shape[0], 0].set(flat_dtb)
    y = None
    for reverse in (False, True):
        def blk(s, reverse=reverse):
            return (0, _scan_block(s, n_ctx, n_all, reverse), 0)

        def blk_t(s, reverse=reverse):
            return (0, 0, _scan_block(s, n_ctx, n_all, reverse))

        def prev(s, reverse=reverse):
            return (0, jnp.maximum(_scan_block(s, n_ctx, n_all, reverse) * per - 1, 0), 0)

        def nxt(s, reverse=reverse):
            return (0, jnp.minimum((_scan_block(s, n_ctx, n_all, reverse) + 1) * per, n8 - 1), 0)

        in_specs = [pl.BlockSpec((bsz, c, pw), blk),
                    pl.BlockSpec((bsz, HALO, 3 * w), prev),
                    pl.BlockSpec((bsz, HALO, 3 * w), nxt),
                    pl.BlockSpec((bsz, nt, c), blk_t),
                    _const_spec((CONV_K, 3 * w)),
                    _const_spec((1, LANES)), _const_spec((1, LANES)),
                    _const_spec((nt, 1)), _const_spec((nt, 1)),
                    _const_spec((1, HEAD_DK))]
        args = [p, p, p, abt, conv_w, alog_r, dtb_r, alog_c, dtb_c, norm_g.reshape(1, HEAD_DK)]
        if reverse:
            in_specs.append(pl.BlockSpec((bsz, c, w), blk))
            args.append(y)
        y = pl.pallas_call(
            functools.partial(_gdn_kernel, reverse=reverse, ctx_len=ctx_len, total=total,
                              n_ctx=n_ctx, n_all=n_all),
            grid=(n_all,),
            in_specs=in_specs,
            out_specs=pl.BlockSpec((bsz, c, w), blk),
            out_shape=jax.ShapeDtypeStruct((bsz, total, w), F32),
            scratch_shapes=[pltpu.VMEM((bsz * N_HEADS, HEAD_DK, HEAD_DK), F32)],
            compiler_params=_params(1),
            name="gdn_bwd" if reverse else "gdn_fwd",
        )(*args)
    return y


def _rwkv_kernel(x_ref, xp_ref, xn_ref, mu_ref, w0_ref, w2_ref, a0_ref, a2_ref, kk_ref, ka_ref, hs_ref, *rest,
                 reverse, ctx_len, total, n_ctx, n_all):
    if reverse:
        g2_ref, rk_ref, lng_ref, lnb_ref, yf_ref, o_ref, buf_ref, s_ref = rest
    else:
        o_ref, buf_ref, s_ref = rest
    c = RWKV_CHUNK
    step = pl.program_id(0)
    blk = _scan_block(step, n_ctx, n_all, reverse)
    bsz = x_ref.shape[0]
    n_pairs = s_ref.shape[0] // bsz
    bw = n_pairs * LANES

    @pl.when(step == 0)
    def _():
        s_ref[...] = jnp.zeros_like(s_ref)

    rows = []
    for b in range(bsz):
        xb = x_ref[b]
        _fill_halo(buf_ref.at[b], xb, xp_ref[b], xn_ref[b], blk * c, c, ctx_len, total)
        rows.append(xb + (0.5 * (_shifted(buf_ref.at[b], -1, c) + _shifted(buf_ref.at[b], 1, c)) - xb) * mu_ref[...])
    x = jnp.concatenate(rows, axis=0)
    n = bsz * c
    r = x[:, 0:bw]
    k = x[:, bw:2 * bw]
    v = x[:, 2 * bw:3 * bw]
    off = 3 * bw
    gd = x[:, off:off + RWKV_GATE_RANK]
    off += RWKV_GATE_RANK
    wd = [x[:, off + d * RWKV_RANK:off + (d + 1) * RWKV_RANK] for d in range(N_DIR)]
    off += N_DIR * RWKV_RANK
    ad = [x[:, off + d * RWKV_RANK:off + (d + 1) * RWKV_RANK] for d in range(N_DIR)]

    def per_head_sum(t):
        t1, t2, t3 = _split3(t)
        hs = hs_ref[...]
        return (jnp.dot(t1, hs, preferred_element_type=F32) + jnp.dot(t2, hs, preferred_element_type=F32)
                + jnp.dot(t3, hs, preferred_element_type=F32))

    def a_of(d):
        return _sigmoid(a0_ref[d:d + 1, :] + _mm(ad[d], a2_ref[d]))

    d = int(reverse)
    w_raw = -_softplus(-(w0_ref[d:d + 1, :] + _mm(jnp.tanh(wd[d]), w2_ref[d]))) - 0.5
    logw = -jnp.exp(w_raw)
    a_dir = a_of(d)
    kkv = k * kk_ref[...]
    kkv = kkv * lax.rsqrt(per_head_sum(kkv * kkv) + EPS)
    kd = k * (1.0 + (a_dir - 1.0) * ka_ref[...])

    cshift = c.bit_length() - 1

    def chunk_masks(size):
        ii = _iota2((size, size), 0)
        jj = _iota2((size, size), 1)
        same = (ii >> cshift) == (jj >> cshift)
        return ((ii <= jj) if reverse else (ii >= jj)) & same, ((ii < jj) if reverse else (ii > jj)) & same

    cum = _tri_mm(jnp.where(chunk_masks(n)[0], 1.0, 0.0), logw)
    last = 0 if reverse else c - 1
    cum_tot = jnp.concatenate([jnp.broadcast_to(cum[b * c + last:b * c + last + 1, :], (c, bw))
                               for b in range(bsz)], axis=0)
    w_inv = jnp.exp(-cum)
    w_rest = jnp.exp(cum_tot - cum)
    alpha_t = -kkv * jnp.exp(cum - logw)
    beta = kkv * a_dir
    beta_t = beta * w_inv
    k_t = kd * w_inv
    r_t = r * jnp.exp(cum)
    beta_h = beta * w_rest
    k_h = kd * w_rest

    n2 = 2 * c
    incl, strict = chunk_masks(n2)
    lane = _iota2((1, LANES), 1)
    m0 = jnp.where(lane < RWKV_N, 1.0, 0.0)
    m1 = 1.0 - m0
    incl2 = jnp.concatenate([incl, incl], axis=1)

    def stack(t, b, p):
        t = t[b * c:(b + 1) * c, p * LANES:(p + 1) * LANES]
        return jnp.concatenate([t * m0, t * m1], axis=0)

    chains = [(b, p) for b in range(bsz) for p in range(n_pairs)]
    states = [s_ref[b * n_pairs + p] for b, p in chains]
    ar = [jnp.concatenate([stack(alpha_t, b, p), stack(r_t, b, p)], axis=0) for b, p in chains]
    bk = [jnp.concatenate([stack(beta_t, b, p), stack(k_t, b, p)], axis=0) for b, p in chains]
    vs = [stack(v, b, p) for b, p in chains]
    bkh = [jnp.concatenate([stack(beta_h, b, p), stack(k_h, b, p)], axis=0) for b, p in chains]
    g = [_mm_nt(x1, x2) for x1, x2 in zip(ar, bk)]
    t_invs = _tri_inverse([jnp.where(strict, -m[0:n2, 0:n2], 0.0) for m in g], c)
    mv = [_mm(jnp.where(strict, m[0:n2, n2:2 * n2], 0.0), vv) for m, vv in zip(g, vs)]
    xs = [_mm_nt(x1, st) for x1, st in zip(ar, states)]
    us = [_mm(t, x1[0:n2] + m) for t, x1, m in zip(t_invs, xs, mv)]
    uv = [jnp.concatenate([u, vv], axis=0) for u, vv in zip(us, vs)]
    ys = [x1[n2:2 * n2] + _mm(jnp.where(incl2, m[n2:2 * n2, :], 0.0), w) for x1, m, w in zip(xs, g, uv)]
    for i, (b, p) in enumerate(chains):
        w_tot = jnp.exp(cum_tot[b * c:b * c + 1, p * LANES:(p + 1) * LANES])
        s_ref[b * n_pairs + p] = states[i] * w_tot + _mm_tn(uv[i], bkh[i])
    y = jnp.concatenate([jnp.concatenate([ys[b * n_pairs + p][0:c] + ys[b * n_pairs + p][c:n2]
                                          for p in range(n_pairs)], axis=1) for b in range(bsz)], axis=0)
    if reverse:
        y = y + jnp.concatenate([yf_ref[b] for b in range(bsz)], axis=0)
        mean = per_head_sum(y) * (1.0 / RWKV_N)
        yc = y - mean
        var = per_head_sum(yc * yc) * (1.0 / RWKV_N)
        yn = yc * lax.rsqrt(var + RWKV_LN_EPS) * lng_ref[...] + lnb_ref[...]
        k_sum = kd + k * (1.0 + (a_of(0) - 1.0) * ka_ref[...])
        bonus = per_head_sum(r * k_sum * rk_ref[...]) * v
        y = (yn + bonus) * _mm(_sigmoid(gd), g2_ref[...])
    for b in range(bsz):
        o_ref[b] = y[b * c:(b + 1) * c]


def _rwkv7(p, mu, w0, w2, a0, a2, g2, k_k, k_a, r_k, ln_g, ln_b, ctx_len):
    bsz, total, pw = p.shape
    bw = k_k.shape[0]
    c = RWKV_CHUNK
    n_all, n_ctx = total // c, ctx_len // c
    per = c // HALO
    n8 = total // HALO
    head_id = jnp.arange(bw) // RWKV_N
    head_sum = (head_id[:, None] == head_id[None, :]).astype(BF16)
    y = None
    for reverse in (False, True):
        def blk(s, reverse=reverse):
            return (0, _scan_block(s, n_ctx, n_all, reverse), 0)

        def prev(s, reverse=reverse):
            return (0, jnp.maximum(_scan_block(s, n_ctx, n_all, reverse) * per - 1, 0), 0)

        def nxt(s, reverse=reverse):
            return (0, jnp.minimum((_scan_block(s, n_ctx, n_all, reverse) + 1) * per, n8 - 1), 0)

        in_specs = [pl.BlockSpec((bsz, c, pw), blk),
                    pl.BlockSpec((bsz, HALO, pw), prev),
                    pl.BlockSpec((bsz, HALO, pw), nxt),
                    _const_spec((1, pw)),
                    _const_spec((N_DIR, bw)), _const_spec((N_DIR, RWKV_RANK, bw)),
                    _const_spec((N_DIR, bw)), _const_spec((N_DIR, RWKV_RANK, bw)),
                    _const_spec((1, bw)), _const_spec((1, bw)), _const_spec((bw, bw))]
        args = [p, p, p, mu.reshape(1, pw), w0, w2.astype(BF16), a0, a2.astype(BF16),
                k_k.reshape(1, bw), k_a.reshape(1, bw), head_sum]
        if reverse:
            in_specs += [_const_spec((RWKV_GATE_RANK, bw)), _const_spec((1, bw)), _const_spec((1, bw)),
                         _const_spec((1, bw)), pl.BlockSpec((bsz, c, bw), blk)]
            args += [g2.astype(BF16), r_k.reshape(1, bw), ln_g.reshape(1, bw), ln_b.reshape(1, bw), y]
        y = pl.pallas_call(
            functools.partial(_rwkv_kernel, reverse=reverse, ctx_len=ctx_len, total=total,
                              n_ctx=n_ctx, n_all=n_all),
            grid=(n_all,),
            in_specs=in_specs,
            out_specs=pl.BlockSpec((bsz, c, bw), blk),
            out_shape=jax.ShapeDtypeStruct((bsz, total, bw), F32),
            scratch_shapes=[pltpu.VMEM((bsz, c + 2 * HALO, pw), F32),
                            pltpu.VMEM((bsz * (bw // LANES), LANES, LANES), F32)],
            compiler_params=_params(1),
            name="rwkv_bwd" if reverse else "rwkv_fwd",
        )(*args)
    return y


def _merge_kernel(h_ref, xm_ref, mod_ref, wg_ref, gb_ref, y0_ref, y1_ref, y2_ref, y3_ref, bw_ref, ow_ref,
                  o_ref, *, rows, ctx_len):
    tile = pl.program_id(1)
    h = h_ref[0]
    d = h.shape[1]
    xm = xm_ref[0]
    merged = jnp.zeros_like(h)
    for k, y_ref in enumerate((y0_ref, y1_ref, y2_ref, y3_ref)):
        gate = _sigmoid(jnp.dot(xm, wg_ref[:, k * d:(k + 1) * d], preferred_element_type=F32)
                        + gb_ref[:, k * d:(k + 1) * d])
        merged = merged + gate * _mm(y_ref[0], bw_ref[k])
    o_ref[0] = h + _mod_row(mod_ref, tile, rows, ctx_len, 2) * _mm(merged, ow_ref[...])


def _merge(h, xm, modtab, w_gates, gate_b, ys, branch_w, out_w, ctx_len):
    bsz, total, d = h.shape
    rows = _token_tile(total, TOKEN_TILES)
    bwid = ys[0].shape[2]
    row_spec = lambda width: pl.BlockSpec((1, rows, width), lambda b, i: (b, i, 0))
    return pl.pallas_call(
        functools.partial(_merge_kernel, rows=rows, ctx_len=ctx_len),
        grid=(bsz, total // rows),
        in_specs=[row_spec(d), row_spec(d),
                  pl.BlockSpec((1, 2, 6, d), lambda b, i: (b, 0, 0, 0)),
                  _weight_spec((d, N_BRANCH * d)), _const_spec((1, N_BRANCH * d)),
                  row_spec(bwid), row_spec(bwid), row_spec(bwid), row_spec(bwid),
                  _weight_spec((N_BRANCH, bwid, d)), _weight_spec((d, d))],
        out_specs=row_spec(d),
        out_shape=jax.ShapeDtypeStruct((bsz, total, d), F32),
        compiler_params=_params(),
        name="merge",
    )(h, xm, modtab, w_gates, gate_b.reshape(1, -1), *ys, branch_w, out_w)


def _ffn_kernel(h_ref, g_ref, mod_ref, w1_ref, w3_ref, w2_ref, fg_ref, o_ref, *, rows, ctx_len, skip, final):
    tile = pl.program_id(1) + skip
    h = h_ref[0]
    u = _norm_modulate(h, g_ref[...], mod_ref, tile, rows, ctx_len, 3, 4).astype(BF16)
    dff = w1_ref.shape[1]
    part = dff // FFN_SPLIT
    out = jnp.zeros_like(h)
    for j in range(FFN_SPLIT):
        cols = slice(j * part, (j + 1) * part)
        t = (_silu(jnp.dot(u, w1_ref[:, cols], preferred_element_type=F32))
             * jnp.dot(u, w3_ref[:, cols], preferred_element_type=F32))
        out = out + _mm(t, w2_ref[cols, :])
    h = h + _mod_row(mod_ref, tile, rows, ctx_len, 5) * out
    if final:
        h = h * lax.rsqrt(jnp.mean(h * h, axis=-1, keepdims=True) + EPS) * fg_ref[...]
    o_ref[0] = h


def _ffn(h, g, modtab, w1, w3, w2, final_g, ctx_len, final):
    bsz, total, d = h.shape
    rows = _token_tile(ctx_len, FFN_TILES)
    dff = w1.shape[1]
    skip = ctx_len // rows if final else 0
    out_rows = total - skip * rows
    return pl.pallas_call(
        functools.partial(_ffn_kernel, rows=rows, ctx_len=ctx_len, skip=skip, final=final),
        grid=(bsz, out_rows // rows),
        in_specs=[pl.BlockSpec((1, rows, d), lambda b, i: (b, i + skip, 0)), _const_spec((1, d)),
                  pl.BlockSpec((1, 2, 6, d), lambda b, i: (b, 0, 0, 0)),
                  _weight_spec((d, dff)), _weight_spec((d, dff)), _weight_spec((dff, d)), _const_spec((1, d))],
        out_specs=pl.BlockSpec((1, rows, d), lambda b, i: (b, i, 0)),
        out_shape=jax.ShapeDtypeStruct((bsz, out_rows, d), F32),
        compiler_params=_params(),
        name="ffn",
    )(h, g.reshape(1, d), modtab, w1, w3, w2, final_g.reshape(1, d))


def _rope_tables(seq, ctx_len):
    t = jnp.arange(seq)
    row = (t // GRID_W).astype(F32)
    col = (t % GRID_W).astype(F32)
    quarter = HEAD_DK // 4
    inv = ROPE_BASE ** (-jnp.arange(quarter, dtype=F32) / quarter)
    ang = jnp.concatenate([row[:, None] * inv, col[:, None] * inv], axis=-1)
    cos, sin = jnp.cos(ang), jnp.sin(ang)
    cos = jnp.concatenate([cos, cos], axis=-1)
    sin = jnp.concatenate([-sin, sin], axis=-1)
    cos = jnp.concatenate([jnp.ones((ctx_len, HEAD_DK), F32), cos], axis=0)
    sin = jnp.concatenate([jnp.zeros((ctx_len, HEAD_DK), F32), sin], axis=0)
    return cos, sin


def kernel(x, c, ctx, c_ctx, mod_w, mod_b, norm1_g, norm2_g, in_w, gate_b, ret_decay_exp, lru_conv_w,
           lru_conv_b, lru_gate_w, lru_gate_b, lru_lambda, gdn_conv_w, gdn_a_log, gdn_dt_bias, gdn_norm_g,
           rwkv_mu, rwkv_w0, rwkv_w2, rwkv_a0, rwkv_a2, rwkv_g2, rwkv_k_k, rwkv_k_a, rwkv_r_k, rwkv_ln_g,
           rwkv_ln_b, branch_w, out_w, ffn_w1, ffn_w3, ffn_w2, final_norm_g):
    bsz, seq, d = x.shape
    ctx_len = ctx.shape[1]
    depth = mod_w.shape[0]
    bw = d // 2
    assert ctx_len % LRU_BLOCK_ROWS == 0 and seq % LRU_BLOCK_ROWS == 0 and bsz + 1 <= HALO

    h = jnp.concatenate([ctx, x], axis=1)
    cond = jnp.zeros((HALO, d), F32).at[:bsz].set(c).at[bsz].set(c_ctx)
    mods = _modulation(cond, mod_w, mod_b).reshape(depth, HALO, 6, d)
    cos, sin = _rope_tables(seq, ctx_len)

    widths = (4 * bw, 2 * bw, 3 * bw, bw, N_DIR * N_HEADS, N_DIR * N_HEADS,
              3 * bw + RWKV_GATE_RANK + 2 * N_DIR * RWKV_RANK, N_BRANCH * d)
    offs = [0]
    for wdt in widths:
        offs.append(offs[-1] + wdt)

    for l in range(depth):
        lat = mods[l, :bsz]
        modtab = jnp.stack([jnp.broadcast_to(mods[l, bsz], lat.shape), lat], axis=1)
        wl = in_w[l]
        w_ret = wl[:, offs[0]:offs[1]].astype(BF16)
        w_lru = wl[:, offs[1]:offs[2]].astype(BF16)
        w_ab = wl[:, offs[4]:offs[6]]
        w_gdn = jnp.concatenate([wl[:, offs[2]:offs[4]], w_ab,
                                 jnp.zeros((d, LANES - w_ab.shape[1]), F32)], axis=1).astype(BF16)
        w_abt = w_ab.T.astype(BF16)
        w_rwkv = wl[:, offs[6]:offs[7]].astype(BF16)
        w_gates = wl[:, offs[7]:offs[8]].astype(BF16)

        xm = _mixer_input(h, norm1_g[l], modtab, ctx_len)
        p_ret = _in_projection(xm, w_ret)
        p_lru = _in_projection(xm, w_lru)
        p_gdn, abt = _in_projection(xm, w_gdn, w_t=w_abt)
        p_rwkv = _in_projection(xm, w_rwkv)

        log_gamma = jnp.log1p(-jnp.exp2(-ret_decay_exp[l].astype(F32)))
        lg_rows = jnp.repeat(log_gamma, HEAD_DK, axis=1).reshape(N_DIR, 1, N_HEADS * HEAD_DK)
        y_ret = _retention(p_ret, cos, sin, lg_rows, ctx_len)
        y_lru = _rglru(p_lru, lru_conv_w[l], lru_conv_b[l], lru_gate_w[l], lru_gate_b[l], lru_lambda[l], ctx_len)
        y_gdn = _gated_deltanet(p_gdn, abt, gdn_conv_w[l], gdn_a_log[l], gdn_dt_bias[l], gdn_norm_g[l], ctx_len)
        y_rwkv = _rwkv7(p_rwkv, rwkv_mu[l], rwkv_w0[l], rwkv_w2[l], rwkv_a0[l], rwkv_a2[l], rwkv_g2[l],
                        rwkv_k_k[l], rwkv_k_a[l], rwkv_r_k[l].reshape(-1), rwkv_ln_g[l], rwkv_ln_b[l], ctx_len)

        h = _merge(h, xm, modtab, w_gates, gate_b[l], (y_ret, y_lru, y_gdn, y_rwkv),
                   branch_w[l].astype(BF16), out_w[l].astype(BF16), ctx_len)
        h = _ffn(h, norm2_g[l], modtab, ffn_w1[l].astype(BF16), ffn_w3[l].astype(BF16),
                 ffn_w2[l].astype(BF16), final_norm_g, ctx_len, final=(l == depth - 1))
    return h
```
